```python
import jax
import jax.numpy as jnp
from jax import lax
import numpy as np

D_MODEL = 4096
BATCH = 4
SEQ = 4096
DEPTH = 2

N_HEADS = 16
HEAD_DIM = 128
N_KV_GROUPS = 4
HEADS_PER_GROUP = N_HEADS // N_KV_GROUPS
A_WIDTH = N_HEADS * HEAD_DIM
KV_WIDTH = N_KV_GROUPS * HEAD_DIM
N_NSA_BRANCHES = 3
CMP_BLOCK = 32
CMP_STRIDE = 16
CMP_HIDDEN = HEAD_DIM
SLC_BLOCK = 64
N_SELECT = 16
WINDOW = 512
WIN_QBLOCK = 128
SLC_QCHUNK = 64
ATTN_SCALE = HEAD_DIM ** -0.5
FORCE_SCORE = 1e9
NEG_INF = -1e30

B_WIDTH = D_MODEL // 4
CONV_WIDTH = 31

C_WIDTH = D_MODEL // 4
SGU_CHUNK = 128
SGU_GROUP_WIDTH = 128
SGU_GROUPS = C_WIDTH // SGU_GROUP_WIDTH

N_BRANCHES = 3
RMS_EPS = 1e-6
LN_EPS = 1e-5

IN_SPLITS = (A_WIDTH, 6 * KV_WIDTH, N_NSA_BRANCHES * N_HEADS, A_WIDTH,
             2 * B_WIDTH, B_WIDTH, 2 * C_WIDTH, C_WIDTH, N_BRANCHES * D_MODEL)
IN_WIDTH = sum(IN_SPLITS)

kernel_name = 'hybrid_nsa_conv_sgu_block'


def rmsnorm(x, g):
    xf = x.astype(jnp.float32)
    y = xf * lax.rsqrt(jnp.mean(xf * xf, axis=-1, keepdims=True) + RMS_EPS)
    return (y * g.astype(jnp.float32)).astype(x.dtype)


def layernorm(x, g, b):
    xf = x.astype(jnp.float32)
    mu = jnp.mean(xf, axis=-1, keepdims=True)
    var = jnp.mean(jnp.square(xf - mu), axis=-1, keepdims=True)
    y = (xf - mu) * lax.rsqrt(var + LN_EPS) * g.astype(jnp.float32) + b.astype(jnp.float32)
    return y.astype(x.dtype)


def masked_softmax(s, mask):
    p = jax.nn.softmax(jnp.where(mask, s, NEG_INF), axis=-1)
    return jnp.where(mask, p, 0.0)


def alibi_slopes():
    h = jnp.arange(1, N_HEADS + 1, dtype=jnp.float32)
    return jnp.exp2(-8.0 * h / N_HEADS).reshape(N_KV_GROUPS, HEADS_PER_GROUP)


def compress(kv, pos, w1, w2):
    B, S, G, dh = kv.shape
    n_cmp = (S - CMP_BLOCK) // CMP_STRIDE + 1
    idx = jnp.arange(n_cmp)[:, None] * CMP_STRIDE + jnp.arange(CMP_BLOCK)[None, :]
    blocks = kv[:, idx] + pos[None, None, :, None, :]
    flat = jnp.moveaxis(blocks, 3, 2).reshape(B, n_cmp, G, CMP_BLOCK * dh)
    return jax.nn.silu(flat @ w1) @ w2


def compressed_attention(q, k_c, v_c, t_pos, slopes):
    n_cmp = k_c.shape[1]
    end = jnp.arange(n_cmp) * CMP_STRIDE + CMP_BLOCK - 1
    dist = t_pos[:, None] - end[None, :]
    mask = dist >= 0
    s = jnp.einsum('bsghd,bngd->bghsn', q, k_c, preferred_element_type=jnp.float32) * ATTN_SCALE
    s = s - slopes[None, :, :, None, None] * dist.astype(jnp.float32)
    p = masked_softmax(s, mask)
    o = jnp.einsum('bghsn,bngd->bsghd', p.astype(v_c.dtype), v_c)
    return o, p


def select_blocks(p_cmp, t_pos):
    S = t_pos.shape[0]
    n_cmp = p_cmp.shape[-1]
    n_sblk = S // SLC_BLOCK
    n_sel = min(N_SELECT, n_sblk)
    c_start = jnp.arange(n_cmp) * CMP_STRIDE
    s_start = jnp.arange(n_sblk) * SLC_BLOCK
    overlap = ((c_start[:, None] < s_start[None, :] + SLC_BLOCK)
               & (c_start[:, None] + CMP_BLOCK > s_start[None, :])).astype(jnp.float32)
    imp = jnp.einsum('bghsn,nj->bgsj', p_cmp, overlap)
    cur = (t_pos // SLC_BLOCK)[:, None]
    blk = jnp.arange(n_sblk)[None, :]
    forced = (blk == 0) | (blk == cur) | (blk == cur - 1)
    imp = jnp.where(forced, FORCE_SCORE, jnp.where(blk <= cur, imp, NEG_INF))
    return lax.top_k(imp, n_sel)[1]


def selected_attention(q, k, v, sel, slopes):
    B, S, G, Hg, dh = q.shape
    n_sblk = S // SLC_BLOCK
    n_sel = sel.shape[-1]
    m = n_sel * SLC_BLOCK
    k_blk = k.reshape(B, n_sblk, SLC_BLOCK, G, dh).transpose(0, 3, 1, 2, 4)
    v_blk = v.reshape(B, n_sblk, SLC_BLOCK, G, dh).transpose(0, 3, 1, 2, 4)
    n_chunk = S // SLC_QCHUNK
    q_c = q.reshape(B, n_chunk, SLC_QCHUNK, G, Hg, dh).swapaxes(0, 1)
    sel_c = sel.reshape(B, G, n_chunk, SLC_QCHUNK, n_sel).transpose(2, 0, 1, 3, 4)
    t_c = jnp.arange(S).reshape(n_chunk, SLC_QCHUNK)
    b_ix = jnp.arange(B)[:, None, None, None]
    g_ix = jnp.arange(G)[None, :, None, None]

    def body(args):
        qc, sc, tc = args
        kg = k_blk[b_ix, g_ix, sc].reshape(B, G, SLC_QCHUNK, m, dh)
        vg = v_blk[b_ix, g_ix, sc].reshape(B, G, SLC_QCHUNK, m, dh)
        pos = (sc[..., None] * SLC_BLOCK + jnp.arange(SLC_BLOCK)).reshape(B, G, SLC_QCHUNK, m)
        dist = (tc[None, None, :, None] - pos)[:, :, None]
        s = jnp.einsum('bqghd,bgqmd->bghqm', qc, kg, preferred_element_type=jnp.float32) * ATTN_SCALE
        s = s - slopes[None, :, :, None, None] * dist.astype(jnp.float32)
        p = masked_softmax(s, dist >= 0)
        return jnp.einsum('bghqm,bgqmd->bqghd', p.astype(vg.dtype), vg)

    o = lax.map(body, (q_c, sel_c, t_c))
    return o.swapaxes(0, 1).reshape(B, S, G, Hg, dh)


def window_attention(q, k, v, slopes):
    B, S, G, Hg, dh = q.shape
    nb = S // WIN_QBLOCK
    n_prev = WINDOW // WIN_QBLOCK
    pad = n_prev * WIN_QBLOCK
    kw_len = (n_prev + 1) * WIN_QBLOCK

    def band(a):
        ap = jnp.pad(a, ((0, 0), (pad, 0), (0, 0), (0, 0))).reshape(B, nb + n_prev, WIN_QBLOCK, G, dh)
        w = jnp.stack([ap[:, i:i + nb] for i in range(n_prev + 1)], axis=2)
        return w.reshape(B, nb, kw_len, G, dh).swapaxes(0, 1)

    k_w, v_w = band(k), band(v)
    q_b = q.reshape(B, nb, WIN_QBLOCK, G, Hg, dh).swapaxes(0, 1)

    def body(args):
        qb, kb, vb, n = args
        t = n * WIN_QBLOCK + jnp.arange(WIN_QBLOCK)
        s_pos = n * WIN_QBLOCK - pad + jnp.arange(kw_len)
        dist = t[:, None] - s_pos[None, :]
        mask = (dist >= 0) & (dist < WINDOW) & (s_pos[None, :] >= 0)
        s = jnp.einsum('bqghd,bkgd->bghqk', qb, kb, preferred_element_type=jnp.float32) * ATTN_SCALE
        s = s - slopes[None, :, :, None, None] * dist.astype(jnp.float32)
        p = masked_softmax(s, mask)
        return jnp.einsum('bghqk,bkgd->bqghd', p.astype(vb.dtype), vb)

    o = lax.map(body, (q_b, k_w, v_w, jnp.arange(nb)))
    return o.swapaxes(0, 1).reshape(B, S, G, Hg, dh)


def nsa_mixer(q, kv, gate_logits, z, cmp_pos, cmp_w1, cmp_w2, w_o):
    B, S, _ = q.shape
    q = q.reshape(B, S, N_KV_GROUPS, HEADS_PER_GROUP, HEAD_DIM)
    kv = kv.reshape(B, S, 6, N_KV_GROUPS, HEAD_DIM)
    k_cr, v_cr, k_slc, v_slc, k_win, v_win = (kv[:, :, i] for i in range(6))
    slopes = alibi_slopes()
    t_pos = jnp.arange(S)
    k_cmp = compress(k_cr, cmp_pos[0], cmp_w1[0], cmp_w2[0])
    v_cmp = compress(v_cr, cmp_pos[1], cmp_w1[1], cmp_w2[1])
    o_cmp, p_cmp = compressed_attention(q, k_cmp, v_cmp, t_pos, slopes)
    sel = select_blocks(p_cmp, t_pos)
    o_slc = selected_attention(q, k_slc, v_slc, sel, slopes)
    o_win = window_attention(q, k_win, v_win, slopes)
    g = jax.nn.sigmoid(gate_logits).reshape(B, S, N_NSA_BRANCHES, N_KV_GROUPS, HEADS_PER_GROUP, 1)
    o = g[:, :, 0] * o_cmp + g[:, :, 1] * o_slc + g[:, :, 2] * o_win
    o = o.reshape(B, S, A_WIDTH) * jax.nn.silu(z)
    return o @ w_o


def conv_module(a, z, conv_w, conv_b, ln_g, ln_b):
    u, gate = jnp.split(a, 2, axis=-1)
    y = u * jax.nn.sigmoid(gate)
    y = lax.conv_general_dilated(y, conv_w[:, None, :], window_strides=(1,),
                                 padding=[(CONV_WIDTH - 1, 0)],
                                 dimension_numbers=('NWC', 'WIO', 'NWC'),
                                 feature_group_count=B_WIDTH) + conv_b
    y = jax.nn.silu(layernorm(y, ln_g, ln_b))
    return y * jax.nn.silu(z)


def sgu_module(uv, z, ln_g, ln_b, w_s, b_s):
    u, v = jnp.split(uv, 2, axis=-1)
    v = layernorm(v, ln_g, ln_b)
    B, S, _ = v.shape
    n_chunk = S // SGU_CHUNK
    v = v.reshape(B, n_chunk, SGU_CHUNK, SGU_GROUPS, SGU_GROUP_WIDTH)
    w = w_s * jnp.tril(jnp.ones((SGU_CHUNK, SGU_CHUNK), w_s.dtype))
    s = jnp.einsum('gts,bnsgc->bntgc', w, v) + b_s.T[None, None, :, :, None]
    return u * s.reshape(B, S, C_WIDTH) * jax.nn.silu(z)


def setup_inputs(seed: int = 0) -> dict:
    key = jax.random.key(seed)
    ks = jax.random.split(key, 20)
    f32 = jnp.float32

    def nrm(k, shape, scale):
        return jax.random.normal(k, shape, f32) * scale

    return {
        'x': nrm(ks[0], (BATCH, SEQ, D_MODEL), 1.0),
        'norm_g': 1.0 + nrm(ks[1], (DEPTH, D_MODEL), 0.02),
        'w_in': nrm(ks[2], (DEPTH, D_MODEL, IN_WIDTH), D_MODEL ** -0.5),
        'cmp_pos': nrm(ks[3], (DEPTH, 2, CMP_BLOCK, HEAD_DIM), 0.1),
        'cmp_w1': nrm(ks[4], (DEPTH, 2, CMP_BLOCK * HEAD_DIM, CMP_HIDDEN), (CMP_BLOCK * HEAD_DIM) ** -0.5),
        'cmp_w2': nrm(ks[5], (DEPTH, 2, CMP_HIDDEN, HEAD_DIM), CMP_HIDDEN ** -0.5),
        'w_o_a': nrm(ks[6], (DEPTH, A_WIDTH, D_MODEL), A_WIDTH ** -0.5),
        'conv_w': nrm(ks[7], (DEPTH, CONV_WIDTH, B_WIDTH), CONV_WIDTH ** -0.5),
        'conv_b': nrm(ks[8], (DEPTH, B_WIDTH), 0.02),
        'conv_ln_g': 1.0 + nrm(ks[9], (DEPTH, B_WIDTH), 0.02),
        'conv_ln_b': nrm(ks[10], (DEPTH, B_WIDTH), 0.02),
        'w_o_b': nrm(ks[11], (DEPTH, B_WIDTH, D_MODEL), B_WIDTH ** -0.5),
        'sgu_ln_g': 1.0 + nrm(ks[12], (DEPTH, C_WIDTH), 0.02),
        'sgu_ln_b': nrm(ks[13], (DEPTH, C_WIDTH), 0.02),
        'sgu_w': nrm(ks[14], (DEPTH, SGU_GROUPS, SGU_CHUNK, SGU_CHUNK), SGU_CHUNK ** -0.5),
        'sgu_b': 1.0 + nrm(ks[15], (DEPTH, SGU_GROUPS, SGU_CHUNK), 0.1),
        'w_o_c': nrm(ks[16], (DEPTH, C_WIDTH, D_MODEL), C_WIDTH ** -0.5),
        'w_out': nrm(ks[17], (DEPTH, D_MODEL, D_MODEL), D_MODEL ** -0.5),
        'final_g': 1.0 + nrm(ks[18], (D_MODEL,), 0.02),
    }


def reference(x, norm_g, w_in, cmp_pos, cmp_w1, cmp_w2, w_o_a, conv_w, conv_b, conv_ln_g, conv_ln_b,
              w_o_b, sgu_ln_g, sgu_ln_b, sgu_w, sgu_b, w_o_c, w_out, final_g):
    B, S, _ = x.shape
    bounds = [int(b) for b in np.cumsum((0,) + IN_SPLITS)]
    for l in range(DEPTH):
        h = rmsnorm(x, norm_g[l])
        w_l = w_in[l]
        (q, kv, nsa_gate, z_a, b_in, z_b, c_in, z_c, merge) = [
            h @ w_l[:, bounds[i]:bounds[i + 1]] for i in range(len(IN_SPLITS))]
        y_a = nsa_mixer(q, kv, nsa_gate, z_a, cmp_pos[l], cmp_w1[l], cmp_w2[l], w_o_a[l])
        y_b = conv_module(b_in, z_b, conv_w[l], conv_b[l], conv_ln_g[l], conv_ln_b[l]) @ w_o_b[l]
        y_c = sgu_module(c_in, z_c, sgu_ln_g[l], sgu_ln_b[l], sgu_w[l], sgu_b[l]) @ w_o_c[l]
        gm = jax.nn.sigmoid(merge).reshape(B, S, N_BRANCHES, D_MODEL)
        y = gm[:, :, 0] * y_a + gm[:, :, 1] * y_b + gm[:, :, 2] * y_c
        x = x + y @ w_out[l]
    return rmsnorm(x, final_g)
```

```python
import functools

import jax
import jax.numpy as jnp
from jax import lax
from jax.experimental import pallas as pl
from jax.experimental.pallas import tpu as pltpu

F32 = jnp.float32
BF16 = jnp.bfloat16

N_HEADS = 16
HEAD_DIM = 128
N_KV_GROUPS = 4
HEADS_PER_GROUP = N_HEADS // N_KV_GROUPS
N_NSA_BRANCHES = 3
CMP_BLOCK = 32
CMP_STRIDE = 16
SLC_BLOCK = 64
N_SELECT = 16
WINDOW = 512
ATTN_SCALE = HEAD_DIM ** -0.5
FORCE_SCORE = 1e9
NEG_INF = -1e30
CONV_WIDTH = 31
SGU_CHUNK = 128
RMS_EPS = 1e-6
LN_EPS = 1e-5

LANES = 128
CONV_HALO = 32
VMEM_LIMIT = 56 * 1024 * 1024


def _params(*semantics):
    return pltpu.CompilerParams(dimension_semantics=semantics, vmem_limit_bytes=VMEM_LIMIT)


def _dot(a, b):
    return jnp.dot(a, b, preferred_element_type=F32)


def _dot_nt(a, b):
    return lax.dot_general(a, b, (((1,), (1,)), ((), ())), preferred_element_type=F32)


def _sigmoid(x):
    return 1.0 / (1.0 + jnp.exp(-x))


def _silu(x):
    return x * _sigmoid(x)


def _rmsnorm_kernel(x_ref, g_ref, o_ref):
    x = x_ref[...]
    ms = jnp.mean(x * x, axis=-1, keepdims=True)
    o_ref[...] = (x * lax.rsqrt(ms + RMS_EPS) * g_ref[...]).astype(o_ref.dtype)


def _rmsnorm(x2d, g, out_dtype, tr=256):
    t, d = x2d.shape
    return pl.pallas_call(
        _rmsnorm_kernel,
        grid=(t // tr,),
        in_specs=[pl.BlockSpec((tr, d), lambda i: (i, 0)),
                  pl.BlockSpec((1, d), lambda i: (0, 0))],
        out_specs=pl.BlockSpec((tr, d), lambda i: (i, 0)),
        out_shape=jax.ShapeDtypeStruct((t, d), out_dtype),
        compiler_params=_params("parallel"),
        name="rmsnorm",
    )(x2d, g.reshape(1, d))


def _proj_kernel(x_ref, w_ref, o_ref, *, act):
    acc = _dot(x_ref[...], w_ref[...])
    if act == "silu":
        acc = _silu(acc)
    elif act == "sigmoid":
        acc = _sigmoid(acc)
    o_ref[...] = acc.astype(o_ref.dtype)


def _proj(x, w, act, out_dtype, tm, tn):
    t, k = x.shape
    n = w.shape[1]
    tm, tn = min(tm, t), min(tn, n)
    return pl.pallas_call(
        functools.partial(_proj_kernel, act=act),
        grid=(n // tn, t // tm),
        in_specs=[pl.BlockSpec((tm, k), lambda j, i: (i, 0)),
                  pl.BlockSpec((k, tn), lambda j, i: (0, j))],
        out_specs=pl.BlockSpec((tm, tn), lambda j, i: (i, j)),
        out_shape=jax.ShapeDtypeStruct((t, n), out_dtype),
        compiler_params=_params("parallel", "parallel"),
        name="proj_" + act,
    )(x, w)


def _glu_kernel(x_ref, wu_ref, wg_ref, o_ref):
    x = x_ref[...]
    o_ref[...] = (_dot(x, wu_ref[...]) * _sigmoid(_dot(x, wg_ref[...]))).astype(o_ref.dtype)


def _glu_proj(x, wu, wg, tm, tn):
    t, k = x.shape
    n = wu.shape[1]
    tm, tn = min(tm, t), min(tn, n)
    wspec = pl.BlockSpec((k, tn), lambda j, i: (0, j))
    return pl.pallas_call(
        _glu_kernel,
        grid=(n // tn, t // tm),
        in_specs=[pl.BlockSpec((tm, k), lambda j, i: (i, 0)), wspec, wspec],
        out_specs=pl.BlockSpec((tm, tn), lambda j, i: (i, j)),
        out_shape=jax.ShapeDtypeStruct((t, n), F32),
        compiler_params=_params("parallel", "parallel"),
        name="proj_glu",
    )(x, wu, wg)


def _compress_kernel(kv_ref, pos_ref, w1_ref, w2_ref, o_ref, *, n_cmp):
    n_pad = o_ref.shape[2]
    half = CMP_BLOCK // 2
    lo = jnp.zeros((n_pad, HEAD_DIM), F32)
    hi = jnp.zeros((n_pad, HEAD_DIM), F32)
    for l in range(half):
        rows = kv_ref[0, pl.ds(l, n_pad, stride=CMP_STRIDE), :]
        a = (rows + pos_ref[0, l:l + 1, :]).astype(BF16)
        b = (rows + pos_ref[0, half + l:half + l + 1, :]).astype(BF16)
        lo = lo + _dot(a, w1_ref[0, l * HEAD_DIM:(l + 1) * HEAD_DIM, :])
        hi = hi + _dot(b, w1_ref[0, (half + l) * HEAD_DIM:(half + l + 1) * HEAD_DIM, :])
    hid = lo + pltpu.roll(hi, n_pad - 1, 0)
    out = _dot(_silu(hid).astype(BF16), w2_ref[0])
    valid = lax.broadcasted_iota(jnp.int32, (n_pad, 1), 0) < n_cmp
    o_ref[0, 0] = jnp.where(valid, out, 0.0).astype(o_ref.dtype)


def _compress(kvc, cmp_pos, cmp_w1, cmp_w2, batch, seq):
    n_pad = seq // CMP_STRIDE
    n_cmp = (seq - CMP_BLOCK) // CMP_STRIDE + 1
    g = N_KV_GROUPS
    return pl.pallas_call(
        functools.partial(_compress_kernel, n_cmp=n_cmp),
        grid=(2, batch, g),
        in_specs=[pl.BlockSpec((1, seq, HEAD_DIM), lambda i, b, gg: (b, 0, i * g + gg)),
                  pl.BlockSpec((1, CMP_BLOCK, HEAD_DIM), lambda i, b, gg: (i, 0, 0)),
                  pl.BlockSpec((1, CMP_BLOCK * HEAD_DIM, HEAD_DIM), lambda i, b, gg: (i, 0, 0)),
                  pl.BlockSpec((1, HEAD_DIM, HEAD_DIM), lambda i, b, gg: (i, 0, 0))],
        out_specs=pl.BlockSpec((1, 1, n_pad, HEAD_DIM), lambda i, b, gg: (i * batch + b, gg, 0, 0)),
        out_shape=jax.ShapeDtypeStruct((2 * batch, g, n_pad, HEAD_DIM), BF16),
        compiler_params=_params("parallel", "parallel", "parallel"),
        name="compress",
    )(kvc, cmp_pos, cmp_w1.astype(BF16), cmp_w2.astype(BF16))


def _attn_kernel(q_ref, kc_ref, vc_ref, ks_ref, vs_ref, kw_ref, vw_ref, gate_ref, z_ref, o_ref,
                 *, tq, tk, n_cmp, seq):
    hg = HEADS_PER_GROUP
    g = pl.program_id(1)
    i = pl.program_id(2)
    t0 = i * tq
    rows = hg * tq
    log_tq = tq.bit_length() - 1
    n_sblk = seq // SLC_BLOCK
    n_sel = min(N_SELECT, n_sblk)

    q4 = jnp.concatenate([q_ref[0, :, h * HEAD_DIM:(h + 1) * HEAD_DIM] for h in range(hg)], axis=0)
    row = lax.broadcasted_iota(jnp.int32, (rows, 1), 0)
    hrow = row >> log_tq
    r_i = row & (tq - 1)
    r_f = r_i.astype(F32)
    slope = jnp.exp2((hg * g + hrow + 1).astype(F32) * -0.5)
    t_i = t0 + r_i

    n_pad = kc_ref.shape[2]
    ncol = lax.broadcasted_iota(jnp.int32, (1, n_pad), 1)
    dist_c = t_i - (ncol * CMP_STRIDE + (CMP_BLOCK - 1))
    mask_c = (dist_c >= 0) & (ncol < n_cmp)
    s = _dot_nt(q4, kc_ref[0, 0]) * ATTN_SCALE - slope * dist_c.astype(F32)
    s = jnp.where(mask_c, s, NEG_INF)
    m = jnp.max(s, axis=-1, keepdims=True)
    p = jnp.where(mask_c, jnp.exp(s - m), 0.0)
    l = jnp.sum(p, axis=-1, keepdims=True)
    inv = 1.0 / jnp.where(l > 0.0, l, 1.0)
    o_cmp = _dot(p.astype(BF16), vc_ref[0, 0]) * inv
    pn = p * inv
    psum = pn[0:tq]
    for h in range(1, hg):
        psum = psum + pn[h * tq:(h + 1) * tq]

    jrow = lax.broadcasted_iota(jnp.int32, (LANES, n_pad), 0)
    nn = lax.broadcasted_iota(jnp.int32, (LANES, n_pad), 1)
    ovl_t = ((nn * CMP_STRIDE < jrow * SLC_BLOCK + SLC_BLOCK)
             & (nn * CMP_STRIDE + CMP_BLOCK > jrow * SLC_BLOCK)
             & (nn < n_cmp) & (jrow < n_sblk)).astype(F32)
    imp_t = lax.dot_general(ovl_t, psum, (((1,), (1,)), ((), ())),
                            precision=lax.Precision.HIGHEST, preferred_element_type=F32)
    imp = imp_t[:n_sblk]
    blk = lax.broadcasted_iota(jnp.int32, (n_sblk, tq), 0)
    cur = (t0 + lax.broadcasted_iota(jnp.int32, (n_sblk, tq), 1)) >> (SLC_BLOCK.bit_length() - 1)
    forced = (blk == 0) | (blk == cur) | (blk == cur - 1)
    val = jnp.where(forced, FORCE_SCORE, jnp.where(blk <= cur, imp, NEG_INF))
    sub = 8
    vals = [val[k * sub:(k + 1) * sub] for k in range(n_sblk // sub)]
    ranks = [jnp.zeros((sub, tq), jnp.int32) for _ in vals]
    sub_row = lax.broadcasted_iota(jnp.int32, (sub, tq), 0)
    for jp in range(n_sblk):
        v_jp = jnp.broadcast_to(val[jp:jp + 1, :], (sub, tq))
        for k, v_k in enumerate(vals):
            if (k + 1) * sub <= jp:
                ahead = jnp.where(v_jp > v_k, 1, 0)
            elif k * sub > jp:
                ahead = jnp.where(v_jp >= v_k, 1, 0)
            else:
                ahead = jnp.where(sub_row + k * sub > jp,
                                  jnp.where(v_jp >= v_k, 1, 0), jnp.where(v_jp > v_k, 1, 0))
            ranks[k] = ranks[k] + ahead
    rank = jnp.concatenate(ranks, axis=0)
    selneg_t = jnp.where(rank < n_sel, 0.0, NEG_INF)
    if n_sblk < LANES:
        selneg_t = jnp.concatenate(
            [selneg_t, jnp.full((LANES - n_sblk, tq), NEG_INF, F32)], axis=0)
    selneg = selneg_t.T.astype(BF16)
    selneg4 = jnp.concatenate([selneg] * hg, axis=0)

    ccol_i = lax.broadcasted_iota(jnp.int32, (1, tk), 1)
    slope_col = slope * ccol_i.astype(F32)
    rel = r_i - ccol_i
    eb = lax.broadcasted_iota(jnp.int32, (LANES, tk), 0)
    ec = lax.broadcasted_iota(jnp.int32, (LANES, tk), 1)

    def step(carry, k, v, p0, extra):
        m_prev, l_prev, acc = carry
        s = _dot_nt(q4, k) * ATTN_SCALE + slope_col
        if extra is not None:
            s = s + extra
        shift = slope * (r_f + (t0 - p0).astype(F32))
        m_new = jnp.maximum(m_prev, jnp.max(s, axis=-1, keepdims=True) - shift)
        alpha = jnp.exp(m_prev - m_new)
        p = jnp.exp(s - (m_new + shift))
        l_new = alpha * l_prev + jnp.sum(p, axis=-1, keepdims=True)
        acc_new = alpha * acc + _dot(p.astype(BF16), v)
        return m_new, l_new, acc_new

    init = (jnp.full((rows, 1), -3e38, F32), jnp.zeros((rows, 1), F32),
            jnp.zeros((rows, HEAD_DIM), F32))

    def sel_bias(p0):
        onehot = (eb == ((p0 + ec) >> (SLC_BLOCK.bit_length() - 1))).astype(BF16)
        return _dot(selneg4, onehot)

    def slc_body(j, carry):
        p0 = pl.multiple_of(j * tk, tk)
        return step(carry, ks_ref[0, pl.ds(p0, tk), :], vs_ref[0, pl.ds(p0, tk), :], p0, sel_bias(p0))

    n_full = (t0 // tk)
    carry = lax.fori_loop(0, n_full, slc_body, init)
    for d in range(tq // tk):
        p0 = pl.multiple_of(t0 + d * tk, tk)
        causal = jnp.where(rel >= d * tk, 0.0, NEG_INF)
        carry = step(carry, ks_ref[0, pl.ds(p0, tk), :], vs_ref[0, pl.ds(p0, tk), :], p0,
                     sel_bias(p0) + causal)
    m_s, l_s, acc_s = carry
    o_slc = acc_s * (1.0 / l_s)

    n_win = WINDOW // tk + tq // tk

    def win_body(jj, carry):
        p0 = t0 - WINDOW + jj * tk
        p0c = pl.multiple_of(jnp.maximum(p0, 0), tk)
        dist = rel + (t0 - p0)
        ok = (dist >= 0) & (dist < WINDOW) & (p0 >= 0)
        new = step(carry, kw_ref[0, pl.ds(p0c, tk), :], vw_ref[0, pl.ds(p0c, tk), :], p0c,
                   jnp.where(ok, 0.0, NEG_INF))
        return new

    carry = lax.fori_loop(0, n_win, win_body, init)
    m_w, l_w, acc_w = carry
    o_win = acc_w * (1.0 / l_w)

    gate = gate_ref[0]
    for h in range(hg):
        sl = slice(h * tq, (h + 1) * tq)
        o_h = (gate[:, h:h + 1] * o_cmp[sl]
               + gate[:, hg + h:hg + h + 1] * o_slc[sl]
               + gate[:, 2 * hg + h:2 * hg + h + 1] * o_win[sl])
        zc = slice(h * HEAD_DIM, (h + 1) * HEAD_DIM)
        o_ref[0, :, zc] = (o_h * z_ref[0, :, zc].astype(F32)).astype(o_ref.dtype)


def _attention(qkv, kvcmp, gates, zs, batch, seq, tq=256, tk=256):
    tq, tk = min(tq, seq), min(tk, seq)
    g = N_KV_GROUPS
    hd = HEAD_DIM
    gw = HEADS_PER_GROUP * hd
    n_pad = seq // CMP_STRIDE
    n_cmp = (seq - CMP_BLOCK) // CMP_STRIDE + 1
    qb = N_HEADS

    def kv_spec(part):
        return pl.BlockSpec((1, seq, hd), lambda b, gg, i: (b, 0, qb + part * g + gg))

    return pl.pallas_call(
        functools.partial(_attn_kernel, tq=tq, tk=tk, n_cmp=n_cmp, seq=seq),
        grid=(batch, g, seq // tq),
        in_specs=[pl.BlockSpec((1, tq, gw), lambda b, gg, i: (b, i, gg)),
                  pl.BlockSpec((1, 1, n_pad, hd), lambda b, gg, i: (b, gg, 0, 0)),
                  pl.BlockSpec((1, 1, n_pad, hd), lambda b, gg, i: (batch + b, gg, 0, 0)),
                  kv_spec(0), kv_spec(1), kv_spec(2), kv_spec(3),
                  pl.BlockSpec((1, tq, LANES), lambda b, gg, i: (b, i, gg)),
                  pl.BlockSpec((1, tq, gw), lambda b, gg, i: (b, i, gg))],
        out_specs=pl.BlockSpec((1, tq, gw), lambda b, gg, i: (b, i, gg)),
        out_shape=jax.ShapeDtypeStruct((batch, seq, N_HEADS * hd), BF16),
        compiler_params=_params("parallel", "parallel", "arbitrary"),
        name="nsa_attention",
    )(qkv, kvcmp, kvcmp, qkv, qkv, qkv, qkv, gates, zs)


def _conv_kernel(cur_ref, halo_ref, w_ref, b_ref, g_ref, beta_ref, z_ref, o_ref, buf_ref, *, rc):
    i = pl.program_id(1)
    ts = cur_ref.shape[1]
    halo = halo_ref[0]
    buf_ref[0:CONV_HALO, :] = jnp.where(i > 0, halo, 0.0)
    buf_ref[CONV_HALO:CONV_HALO + ts, :] = cur_ref[0]
    first = CONV_HALO - (CONV_WIDTH - 1)
    for c in range(ts // rc):
        acc = jnp.zeros((rc, cur_ref.shape[2]), F32)
        for k in range(CONV_WIDTH):
            acc = acc + buf_ref[c * rc + first + k:c * rc + first + k + rc, :] * w_ref[k:k + 1, :]
        y = acc + b_ref[...]
        mu = jnp.mean(y, axis=-1, keepdims=True)
        d = y - mu
        var = jnp.mean(d * d, axis=-1, keepdims=True)
        y = d * lax.rsqrt(var + LN_EPS) * g_ref[...] + beta_ref[...]
        y = _silu(y) * z_ref[0, c * rc:(c + 1) * rc, :].astype(F32)
        o_ref[0, c * rc:(c + 1) * rc, :] = y.astype(o_ref.dtype)


def _conv_module(y_glu, zs, z_block, conv_w, conv_b, ln_g, ln_b, batch, seq, ts=256, rc=32):
    w = y_glu.shape[-1]
    ts = min(ts, seq)
    hb = ts // CONV_HALO
    vec = lambda a: a.reshape(1, w)
    vspec = pl.BlockSpec((1, w), lambda b, i: (0, 0))
    return pl.pallas_call(
        functools.partial(_conv_kernel, rc=rc),
        grid=(batch, seq // ts),
        in_specs=[pl.BlockSpec((1, ts, w), lambda b, i: (b, i, 0)),
                  pl.BlockSpec((1, CONV_HALO, w), lambda b, i: (b, jnp.maximum(i * hb - 1, 0), 0)),
                  pl.BlockSpec((CONV_WIDTH, w), lambda b, i: (0, 0)),
                  vspec, vspec, vspec,
                  pl.BlockSpec((1, ts, w), lambda b, i: (b, i, z_block))],
        out_specs=pl.BlockSpec((1, ts, w), lambda b, i: (b, i, 0)),
        out_shape=jax.ShapeDtypeStruct((batch, seq, w), BF16),
        scratch_shapes=[pltpu.VMEM((CONV_HALO + ts, w), F32)],
        compiler_params=_params("parallel", "parallel"),
        name="conv_module",
    )(y_glu, y_glu, conv_w, vec(conv_b), vec(ln_g), vec(ln_b), zs)


def _sgu_kernel(u_ref, v_ref, z_ref, g_ref, beta_ref, w_ref, bt_ref, o_ref):
    ts, width = v_ref.shape
    n_groups = w_ref.shape[0]
    gw = width // n_groups
    v = v_ref[...]
    mu = jnp.mean(v, axis=-1, keepdims=True)
    d = v - mu
    var = jnp.mean(d * d, axis=-1, keepdims=True)
    vn = (d * lax.rsqrt(var + LN_EPS) * g_ref[...] + beta_ref[...]).astype(BF16)
    tri = (lax.broadcasted_iota(jnp.int32, (SGU_CHUNK, SGU_CHUNK), 0)
           >= lax.broadcasted_iota(jnp.int32, (SGU_CHUNK, SGU_CHUNK), 1))
    for gi in range(n_groups):
        w_g = jnp.where(tri, w_ref[gi], 0.0).astype(BF16)
        cols = slice(gi * gw, (gi + 1) * gw)
        for c in range(ts // SGU_CHUNK):
            rws = slice(c * SGU_CHUNK, (c + 1) * SGU_CHUNK)
            s = _dot(w_g, vn[rws, cols]) + bt_ref[:, gi:gi + 1]
            out = u_ref[rws, cols].astype(F32) * s * z_ref[rws, cols].astype(F32)
            o_ref[rws, cols] = out.astype(o_ref.dtype)


def _sgu_module(u_src, u_block, v_src, v_block, zs, z_block, ln_g, ln_b, sgu_w, sgu_b, ts=512):
    t = v_src.shape[0]
    n_groups = sgu_w.shape[0]
    width = n_groups * sgu_w.shape[1]
    ts = min(ts, t)
    vspec = pl.BlockSpec((1, width), lambda i: (0, 0))
    return pl.pallas_call(
        _sgu_kernel,
        grid=(t // ts,),
        in_specs=[pl.BlockSpec((ts, width), lambda i: (i, u_block)),
                  pl.BlockSpec((ts, width), lambda i: (i, v_block)),
                  pl.BlockSpec((ts, width), lambda i: (i, z_block)),
                  vspec, vspec,
                  pl.BlockSpec(sgu_w.shape, lambda i: (0, 0, 0)),
                  pl.BlockSpec((SGU_CHUNK, n_groups), lambda i: (0, 0))],
        out_specs=pl.BlockSpec((ts, width), lambda i: (i, 0)),
        out_shape=jax.ShapeDtypeStruct((t, width), BF16),
        compiler_params=_params("parallel"),
        name="sgu_module",
    )(u_src, v_src, zs, ln_g.reshape(1, width), ln_b.reshape(1, width), sgu_w, sgu_b.T)


def _merge_kernel(h_ref, oa_ref, ob_ref, oc_ref, wm0_ref, wm1_ref, wm2_ref,
                  woa_ref, wob_ref, woc_ref, y_ref):
    h = h_ref[...]
    y = _sigmoid(_dot(h, wm0_ref[...])) * _dot(oa_ref[...], woa_ref[...])
    y = y + _sigmoid(_dot(h, wm1_ref[...])) * _dot(ob_ref[...], wob_ref[...])
    y = y + _sigmoid(_dot(h, wm2_ref[...])) * _dot(oc_ref[...], woc_ref[...])
    y_ref[...] = y.astype(y_ref.dtype)


def _merge(h, oa, ob, oc, w_merge, w_o_a, w_o_b, w_o_c, tm=512, tn=512):
    t, d = h.shape
    tm, tn = min(tm, t), min(tn, d)
    nb = d // tn

    def act(a):
        return pl.BlockSpec((tm, a.shape[1]), lambda j, i: (i, 0))

    def wm(branch):
        return pl.BlockSpec((d, tn), lambda j, i: (0, branch * nb + j))

    def wo(a):
        return pl.BlockSpec((a.shape[0], tn), lambda j, i: (0, j))

    return pl.pallas_call(
        _merge_kernel,
        grid=(nb, t // tm),
        in_specs=[act(h), act(oa), act(ob), act(oc), wm(0), wm(1), wm(2),
                  wo(w_o_a), wo(w_o_b), wo(w_o_c)],
        out_specs=pl.BlockSpec((tm, tn), lambda j, i: (i, j)),
        out_shape=jax.ShapeDtypeStruct((t, d), BF16),
        compiler_params=_params("parallel", "parallel"),
        name="merge",
    )(h, oa, ob, oc, w_merge, w_merge, w_merge, w_o_a, w_o_b, w_o_c)


def _out_kernel(y_ref, w_ref, x_ref, o_ref):
    o_ref[...] = x_ref[...] + _dot(y_ref[...], w_ref[...])


def _out_proj(y, w, x, tm=1024, tn=512):
    t, d = y.shape
    n = w.shape[1]
    tm, tn = min(tm, t), min(tn, n)
    return pl.pallas_call(
        _out_kernel,
        grid=(n // tn, t // tm),
        in_specs=[pl.BlockSpec((tm, d), lambda j, i: (i, 0)),
                  pl.BlockSpec((d, tn), lambda j, i: (0, j)),
                  pl.BlockSpec((tm, tn), lambda j, i: (i, j))],
        out_specs=pl.BlockSpec((tm, tn), lambda j, i: (i, j)),
        out_shape=jax.ShapeDtypeStruct((t, n), F32),
        compiler_params=_params("parallel", "parallel"),
        name="out_proj",
    )(y, w, x)


def _layer(x2d, batch, seq, norm_g, w_in, cmp_pos, cmp_w1, cmp_w2, w_o_a, conv_w, conv_b,
           conv_ln_g, conv_ln_b, w_o_b, sgu_ln_g, sgu_ln_b, sgu_w, sgu_b, w_o_c, w_out):
    d = x2d.shape[1]
    a_w = w_o_a.shape[0]
    b_w = w_o_b.shape[0]
    c_w = w_o_c.shape[0]
    kv_w = N_KV_GROUPS * HEAD_DIM
    n_gate = N_NSA_BRANCHES * N_HEADS
    hg = HEADS_PER_GROUP

    o_q = 0
    o_kv = o_q + a_w
    o_gate = o_kv + 6 * kv_w
    o_za = o_gate + n_gate
    o_b = o_za + a_w
    o_zb = o_b + 2 * b_w
    o_c = o_zb + b_w
    o_zc = o_c + 2 * c_w
    o_m = o_zc + c_w

    def cols(a, b):
        return w_in[:, a:b]

    w_bf = jnp.concatenate([cols(o_q, o_q + a_w), cols(o_kv + 2 * kv_w, o_kv + 6 * kv_w),
                            cols(o_c, o_c + c_w)], axis=1).astype(BF16)
    w_f32 = jnp.concatenate([cols(o_kv, o_kv + 2 * kv_w), cols(o_c + c_w, o_c + 2 * c_w)],
                            axis=1).astype(BF16)
    w_silu = jnp.concatenate([cols(o_za, o_za + a_w), cols(o_zb, o_zb + b_w),
                              cols(o_zc, o_zc + c_w)], axis=1).astype(BF16)
    wg = cols(o_gate, o_gate + n_gate).reshape(d, N_NSA_BRANCHES, N_KV_GROUPS, hg)
    wg = wg.transpose(0, 2, 1, 3).reshape(d, N_KV_GROUPS, N_NSA_BRANCHES * hg)
    w_gate = jnp.pad(wg, ((0, 0), (0, 0), (0, LANES - N_NSA_BRANCHES * hg)))
    w_gate = w_gate.reshape(d, N_KV_GROUPS * LANES).astype(BF16)
    w_glu_u = cols(o_b, o_b + b_w).astype(BF16)
    w_glu_g = cols(o_b + b_w, o_b + 2 * b_w).astype(BF16)
    w_merge = cols(o_m, o_m + 3 * d).astype(BF16)

    h = _rmsnorm(x2d, norm_g, BF16)
    p_bf = _proj(h, w_bf, "none", BF16, 1024, 1024)
    p_f32 = _proj(h, w_f32, "none", F32, 1024, 1024)
    p_silu = _proj(h, w_silu, "silu", BF16, 1024, 1024)
    p_gate = _proj(h, w_gate, "sigmoid", F32, 1024, 512)
    y_glu = _glu_proj(h, w_glu_u, w_glu_g, 1024, 512)

    def b3(a):
        return a.reshape(batch, seq, a.shape[-1])

    kvcmp = _compress(b3(p_f32), cmp_pos, cmp_w1, cmp_w2, batch, seq)
    o_a = _attention(b3(p_bf), kvcmp, b3(p_gate), b3(p_silu), batch, seq)
    o_b_ = _conv_module(b3(y_glu), b3(p_silu), a_w // b_w, conv_w, conv_b, conv_ln_g, conv_ln_b,
                        batch, seq)
    o_c_ = _sgu_module(p_bf, (a_w + 4 * kv_w) // c_w, p_f32, 2 * kv_w // c_w,
                       p_silu, (a_w + b_w) // c_w, sgu_ln_g, sgu_ln_b, sgu_w, sgu_b)
    y = _merge(h, o_a.reshape(-1, a_w), o_b_.reshape(-1, b_w), o_c_, w_merge,
               w_o_a.astype(BF16), w_o_b.astype(BF16), w_o_c.astype(BF16))
    return _out_proj(y, w_out.astype(BF16), x2d)


def kernel(x, norm_g, w_in, cmp_pos, cmp_w1, cmp_w2, w_o_a, conv_w, conv_b, conv_ln_g, conv_ln_b,
           w_o_b, sgu_ln_g, sgu_ln_b, sgu_w, sgu_b, w_o_c, w_out, final_g):
    batch, seq, d = x.shape
    x2d = x.reshape(batch * seq, d)
    for l in range(norm_g.shape[0]):
        x2d = _layer(x2d, batch, seq, norm_g[l], w_in[l], cmp_pos[l], cmp_w1[l], cmp_w2[l],
                     w_o_a[l], conv_w[l], conv_b[l], conv_ln_g[l], conv_ln_b[l], w_o_b[l],
                     sgu_ln_g[l], sgu_ln_b[l], sgu_w[l], sgu_b[l], w_o_c[l], w_out[l])
    return _rmsnorm(x2d, final_g, x.dtype).reshape(batch, seq, d)
```

```python
import functools

import jax
import jax.numpy as jnp
from jax import lax
from jax.experimental import pallas as pl
from jax.experimental.pallas import tpu as pltpu

F32 = jnp.float32
BF16 = jnp.bfloat16

N_HEADS = 16
HEAD_DIM = 128
N_KV_GROUPS = 4
HEADS_PER_GROUP = N_HEADS // N_KV_GROUPS
N_NSA_BRANCHES = 3
CMP_BLOCK = 32
CMP_STRIDE = 16
SLC_BLOCK = 64
N_SELECT = 16
WINDOW = 512
ATTN_SCALE = HEAD_DIM ** -0.5
FORCE_SCORE = 1e9
NEG_INF = -1e30
CONV_WIDTH = 31
SGU_CHUNK = 128
RMS_EPS = 1e-6
LN_EPS = 1e-5

LANES = 128
CONV_HALO = 32
VMEM_LIMIT = 56 * 1024 * 1024


def _params(*semantics):
    return pltpu.CompilerParams(dimension_semantics=semantics, vmem_limit_bytes=VMEM_LIMIT)


def _dot(a, b):
    return jnp.dot(a, b, preferred_element_type=F32)


def _dot_nt(a, b):
    return lax.dot_general(a, b, (((1,), (1,)), ((), ())), preferred_element_type=F32)


def _sigmoid(x):
    return 1.0 / (1.0 + jnp.exp(-x))


def _silu(x):
    return x * _sigmoid(x)


def _rmsnorm_kernel(x_ref, g_ref, o_ref):
    x = x_ref[...]
    ms = jnp.mean(x * x, axis=-1, keepdims=True)
    o_ref[...] = (x * lax.rsqrt(ms + RMS_EPS) * g_ref[...]).astype(o_ref.dtype)


def _rmsnorm(x2d, g, out_dtype, tr=256):
    t, d = x2d.shape
    return pl.pallas_call(
        _rmsnorm_kernel,
        grid=(t // tr,),
        in_specs=[pl.BlockSpec((tr, d), lambda i: (i, 0)),
                  pl.BlockSpec((1, d), lambda i: (0, 0))],
        out_specs=pl.BlockSpec((tr, d), lambda i: (i, 0)),
        out_shape=jax.ShapeDtypeStruct((t, d), out_dtype),
        compiler_params=_params("parallel"),
        name="rmsnorm",
    )(x2d, g.reshape(1, d))


def _proj_kernel(x_ref, w_ref, o_ref, *, act, scale):
    acc = _dot(x_ref[...], w_ref[...])
    if act == "silu":
        acc = _silu(acc)
    elif act == "sigmoid":
        acc = _sigmoid(acc)
    elif act == "scale":
        acc = acc * scale
    o_ref[...] = acc.astype(o_ref.dtype)


def _proj(x, w, act, out_dtype, tm, tn, scale=None):
    t, k = x.shape
    n = w.shape[1]
    tm, tn = min(tm, t), min(tn, n)
    return pl.pallas_call(
        functools.partial(_proj_kernel, act=act, scale=scale),
        grid=(n // tn, t // tm),
        in_specs=[pl.BlockSpec((tm, k), lambda j, i: (i, 0)),
                  pl.BlockSpec((k, tn), lambda j, i: (0, j))],
        out_specs=pl.BlockSpec((tm, tn), lambda j, i: (i, j)),
        out_shape=jax.ShapeDtypeStruct((t, n), out_dtype),
        compiler_params=_params("parallel", "parallel"),
        name="proj_" + act,
    )(x, w)


def _glu_kernel(x_ref, wu_ref, wg_ref, o_ref):
    x = x_ref[...]
    o_ref[...] = (_dot(x, wu_ref[...]) * _sigmoid(_dot(x, wg_ref[...]))).astype(o_ref.dtype)


def _glu_proj(x, wu, wg, tm, tn):
    t, k = x.shape
    n = wu.shape[1]
    tm, tn = min(tm, t), min(tn, n)
    wspec = pl.BlockSpec((k, tn), lambda j, i: (0, j))
    return pl.pallas_call(
        _glu_kernel,
        grid=(n // tn, t // tm),
        in_specs=[pl.BlockSpec((tm, k), lambda j, i: (i, 0)), wspec, wspec],
        out_specs=pl.BlockSpec((tm, tn), lambda j, i: (i, j)),
        out_shape=jax.ShapeDtypeStruct((t, n), F32),
        compiler_params=_params("parallel", "parallel"),
        name="proj_glu",
    )(x, wu, wg)


def _compress_kernel(kv_ref, pos_ref, w1_ref, w2_ref, o_ref, *, n_cmp):
    n_pad = o_ref.shape[2]
    half = CMP_BLOCK // 2
    lo = jnp.zeros((n_pad, HEAD_DIM), F32)
    hi = jnp.zeros((n_pad, HEAD_DIM), F32)
    for l in range(half):
        rows = kv_ref[0, pl.ds(l, n_pad, stride=CMP_STRIDE), :]
        a = (rows + pos_ref[0, l:l + 1, :]).astype(BF16)
        b = (rows + pos_ref[0, half + l:half + l + 1, :]).astype(BF16)
        lo = lo + _dot(a, w1_ref[0, l * HEAD_DIM:(l + 1) * HEAD_DIM, :])
        hi = hi + _dot(b, w1_ref[0, (half + l) * HEAD_DIM:(half + l + 1) * HEAD_DIM, :])
    hid = lo + pltpu.roll(hi, n_pad - 1, 0)
    out = _dot(_silu(hid).astype(BF16), w2_ref[0])
    valid = lax.broadcasted_iota(jnp.int32, (n_pad, 1), 0) < n_cmp
    o_ref[0, 0] = jnp.where(valid, out, 0.0).astype(o_ref.dtype)


def _compress(kvc, cmp_pos, cmp_w1, cmp_w2, batch, seq):
    n_pad = seq // CMP_STRIDE
    n_cmp = (seq - CMP_BLOCK) // CMP_STRIDE + 1
    g = N_KV_GROUPS
    return pl.pallas_call(
        functools.partial(_compress_kernel, n_cmp=n_cmp),
        grid=(2, batch, g),
        in_specs=[pl.BlockSpec((1, seq, HEAD_DIM), lambda i, b, gg: (b, 0, i * g + gg)),
                  pl.BlockSpec((1, CMP_BLOCK, HEAD_DIM), lambda i, b, gg: (i, 0, 0)),
                  pl.BlockSpec((1, CMP_BLOCK * HEAD_DIM, HEAD_DIM), lambda i, b, gg: (i, 0, 0)),
                  pl.BlockSpec((1, HEAD_DIM, HEAD_DIM), lambda i, b, gg: (i, 0, 0))],
        out_specs=pl.BlockSpec((1, 1, n_pad, HEAD_DIM), lambda i, b, gg: (i * batch + b, gg, 0, 0)),
        out_shape=jax.ShapeDtypeStruct((2 * batch, g, n_pad, HEAD_DIM), BF16),
        compiler_params=_params("parallel", "parallel", "parallel"),
        name="compress",
    )(kvc, cmp_pos, cmp_w1.astype(BF16), cmp_w2.astype(BF16))


LOG2E = 1.4426950408889634
POS_COL = SLC_BLOCK
N_PIECES = 3


def _key_extra(pos):
    n = pos.shape[0]
    col = lax.broadcasted_iota(jnp.int32, (n, HEAD_DIM), 1)
    blk = pos >> (SLC_BLOCK.bit_length() - 1)
    hi = (blk * SLC_BLOCK).astype(F32)
    lo = (pos & (SLC_BLOCK - 1)).astype(F32)
    onehot = jnp.where(col == blk, 1.0, 0.0)
    return jnp.where(col < POS_COL, onehot,
                     jnp.where(col < POS_COL + N_PIECES, hi,
                               jnp.where(col < POS_COL + 2 * N_PIECES, lo, 0.0)))


def _kvprep_kernel(ks_ref, vs_ref, kw_ref, vw_ref, ksa_ref, vst_ref, kwa_ref, vwt_ref):
    ts = ks_ref.shape[1]
    pos = pl.program_id(2) * ts + lax.broadcasted_iota(jnp.int32, (ts, 1), 0)
    extra = _key_extra(pos).astype(BF16)
    for src, dst in ((ks_ref, ksa_ref), (kw_ref, kwa_ref)):
        dst[0, 0, :, 0:HEAD_DIM] = src[0]
        dst[0, 0, :, HEAD_DIM:2 * HEAD_DIM] = extra
    for src, dst in ((vs_ref, vst_ref), (vw_ref, vwt_ref)):
        dst[0, 0] = src[0].astype(F32).T.astype(BF16)


def _kv_prep(kv, batch, seq, ts=512):
    ts = min(ts, seq)
    g = N_KV_GROUPS
    hd = HEAD_DIM
    ka = jax.ShapeDtypeStruct((batch, g, seq, 2 * hd), BF16)
    vt = jax.ShapeDtypeStruct((batch, g, hd, seq), BF16)
    ka_spec = pl.BlockSpec((1, 1, ts, 2 * hd), lambda b, gg, i: (b, gg, i, 0))
    vt_spec = pl.BlockSpec((1, 1, hd, ts), lambda b, gg, i: (b, gg, 0, i))

    def part(p):
        return pl.BlockSpec((1, ts, hd), lambda b, gg, i: (b, i, p * g + gg))

    return pl.pallas_call(
        _kvprep_kernel,
        grid=(batch, g, seq // ts),
        in_specs=[part(0), part(1), part(2), part(3)],
        out_specs=[ka_spec, vt_spec, ka_spec, vt_spec],
        out_shape=[ka, vt, ka, vt],
        compiler_params=_params("parallel", "parallel", "parallel"),
        name="kv_prep",
    )(kv, kv, kv, kv)


def _attn_kernel(q_ref, kc_ref, vc_ref, ksa_ref, vst_ref, kwa_ref, vwt_ref, gate_ref, z_ref, o_ref,
                 kca_ref, vct_ref, *, tq, tk, n_cmp, seq):
    hg = HEADS_PER_GROUP
    g = pl.program_id(1)
    i = pl.program_id(2)
    t0 = i * tq
    n_sblk = seq // SLC_BLOCK
    n_sel = min(N_SELECT, n_sblk)
    n_pad = kc_ref.shape[2]
    nrow = lax.broadcasted_iota(jnp.int32, (n_pad, 1), 0)
    end_c = nrow * CMP_STRIDE + (CMP_BLOCK - 1)

    @pl.when(i == 0)
    def _():
        kca_ref[:, 0:HEAD_DIM] = kc_ref[0, 0]
        kca_ref[:, HEAD_DIM:2 * HEAD_DIM] = _key_extra(end_c).astype(BF16)
        vct_ref[...] = vc_ref[0, 0].astype(F32).T.astype(BF16)

    t_q = t0 + lax.broadcasted_iota(jnp.int32, (1, tq), 1)
    lane = lax.broadcasted_iota(jnp.int32, (1, LANES), 1)
    kpos = lane - POS_COL

    def slope_cols(h):
        slope2 = jnp.exp2(jnp.full((1, LANES), hg * g + (h + 1), jnp.int32).astype(F32) * -0.5) * LOG2E
        piece0 = slope2.astype(BF16).astype(F32)
        rest = slope2 - piece0
        piece1 = rest.astype(BF16).astype(F32)
        piece2 = rest - piece1
        return jnp.where((kpos == 0) | (kpos == N_PIECES), piece0,
                         jnp.where((kpos == 1) | (kpos == N_PIECES + 1), piece1,
                                   jnp.where((kpos == 2) | (kpos == N_PIECES + 2), piece2, 0.0)))

    q_h = [q_ref[0, :, h * HEAD_DIM:(h + 1) * HEAD_DIM] for h in range(hg)]
    slopes = [slope_cols(h) for h in range(hg)]
    qa_plain = [jnp.concatenate([q_h[h], jnp.broadcast_to(slopes[h], (tq, LANES)).astype(BF16)], axis=1)
                for h in range(hg)]

    mask_c = (end_c <= t_q) & (nrow < n_cmp)
    kca = kca_ref[...]
    vct = vct_ref[...]
    o_cmp = []
    psum = jnp.zeros((n_pad, tq), F32)
    for h in range(hg):
        s = jnp.where(mask_c, _dot_nt(kca, qa_plain[h]), NEG_INF)
        m = jnp.max(s, axis=0, keepdims=True)
        p = jnp.where(mask_c, jnp.exp2(s - m), 0.0)
        l = jnp.sum(p, axis=0, keepdims=True)
        inv = 1.0 / jnp.where(l > 0.0, l, 1.0)
        o_cmp.append(_dot(vct, p.astype(BF16)) * inv)
        psum = psum + p * inv

    jrow = lax.broadcasted_iota(jnp.int32, (LANES, n_pad), 0)
    nn = lax.broadcasted_iota(jnp.int32, (LANES, n_pad), 1)
    ovl_t = ((nn * CMP_STRIDE < jrow * SLC_BLOCK + SLC_BLOCK)
             & (nn * CMP_STRIDE + CMP_BLOCK > jrow * SLC_BLOCK)
             & (nn < n_cmp) & (jrow < n_sblk)).astype(F32)
    imp_t = jnp.dot(ovl_t, psum, precision=lax.Precision.HIGHEST, preferred_element_type=F32)
    imp = imp_t[:n_sblk]
    blk = lax.broadcasted_iota(jnp.int32, (n_sblk, tq), 0)
    cur = (t0 + lax.broadcasted_iota(jnp.int32, (n_sblk, tq), 1)) >> (SLC_BLOCK.bit_length() - 1)
    forced = (blk == 0) | (blk == cur) | (blk == cur - 1)
    val = jnp.where(forced, FORCE_SCORE, jnp.where(blk <= cur, imp, NEG_INF))
    sub = 8
    vals = [val[k * sub:(k + 1) * sub] for k in range(n_sblk // sub)]
    ranks = [jnp.zeros((sub, tq), jnp.int32) for _ in vals]
    sub_row = lax.broadcasted_iota(jnp.int32, (sub, tq), 0)
    for jp in range(n_sblk):
        v_jp = jnp.broadcast_to(val[jp:jp + 1, :], (sub, tq))
        for k, v_k in enumerate(vals):
            if (k + 1) * sub <= jp:
                ahead = jnp.where(v_jp > v_k, 1, 0)
            elif k * sub > jp:
                ahead = jnp.where(v_jp >= v_k, 1, 0)
            else:
                ahead = jnp.where(sub_row + k * sub > jp,
                                  jnp.where(v_jp >= v_k, 1, 0), jnp.where(v_jp > v_k, 1, 0))
            ranks[k] = ranks[k] + ahead
    rank = jnp.concatenate(ranks, axis=0)
    selneg_t = jnp.where(rank < n_sel, 0.0, NEG_INF)
    if n_sblk < LANES:
        selneg_t = jnp.concatenate(
            [selneg_t, jnp.zeros((LANES - n_sblk, tq), F32)], axis=0)
    selneg = selneg_t.T
    qa_sel = [jnp.concatenate([q_h[h], jnp.where(lane < POS_COL, selneg, slopes[h]).astype(BF16)],
                              axis=1) for h in range(hg)]

    rel = (lax.broadcasted_iota(jnp.int32, (1, tq), 1)
           - lax.broadcasted_iota(jnp.int32, (tk, 1), 0))

    def run(qa, ka_ref, vt_ref, lo, hi, loop_mask, tail_masks):
        def tile(carry, j, mask_fn):
            p0 = pl.multiple_of(j * tk, tk)
            k_a = ka_ref[0, 0, pl.ds(p0, tk), :]
            v_t = vt_ref[0, 0, :, pl.ds(p0, tk)]
            mask = None if mask_fn is None else mask_fn(t0 - p0)
            sc = [_dot_nt(k_a, qa[h]) for h in range(hg)]
            probs, stats = [], []
            for h in range(hg):
                m_prev, l_prev, _ = carry[h]
                s = sc[h] if mask is None else jnp.where(mask, sc[h], NEG_INF)
                m_new = jnp.maximum(m_prev, jnp.max(s, axis=0, keepdims=True))
                alpha = jnp.exp2(m_prev - m_new)
                p = jnp.exp2(s - m_new)
                probs.append(p.astype(BF16))
                stats.append((m_new, alpha * l_prev + jnp.sum(p, axis=0, keepdims=True), alpha))
            return tuple((m_new, l_new, alpha * carry[h][2] + _dot(v_t, probs[h]))
                         for h, (m_new, l_new, alpha) in enumerate(stats))

        init = tuple((jnp.full((1, tq), -3e38, F32), jnp.zeros((1, tq), F32),
                      jnp.zeros((HEAD_DIM, tq), F32)) for _ in range(hg))
        carry = lax.fori_loop(lo, hi, lambda j, c: tile(c, j, loop_mask), init)
        for d, mask_fn in enumerate(tail_masks):
            carry = tile(carry, hi + d, mask_fn)
        return [acc * (1.0 / l) for _, l, acc in carry]

    def causal(off):
        return rel + off >= 0

    def window(off):
        dist = rel + off
        return (dist >= 0) & (dist < WINDOW)

    o_slc = run(qa_sel, ksa_ref, vst_ref, 0, t0 // tk, None, [causal] * max(tq // tk, 1))

    o_win = run(qa_plain, kwa_ref, vwt_ref, jnp.maximum(t0 - WINDOW + 1, 0) // tk,
                (t0 + tq - 1) // tk, window, [window])

    gate_t = gate_ref[0].T
    for h in range(hg):
        o_t = (gate_t[h:h + 1] * o_cmp[h] + gate_t[hg + h:hg + h + 1] * o_slc[h]
               + gate_t[2 * hg + h:2 * hg + h + 1] * o_win[h])
        zc = slice(h * HEAD_DIM, (h + 1) * HEAD_DIM)
        o_ref[0, :, zc] = (o_t.T * z_ref[0, :, zc].astype(F32)).astype(o_ref.dtype)


def _attention(q, kv, kvcmp, gates, zs, batch, seq, tq=512, tk=512):
    tq, tk = min(tq, seq), min(tk, seq)
    g = N_KV_GROUPS
    hd = HEAD_DIM
    gw = HEADS_PER_GROUP * hd
    n_pad = seq // CMP_STRIDE
    n_cmp = (seq - CMP_BLOCK) // CMP_STRIDE + 1
    assert seq // SLC_BLOCK <= POS_COL and (tq % tk == 0 or tk % tq == 0) and WINDOW % tk == 0
    ksa, vst, kwa, vwt = _kv_prep(kv, batch, seq)
    ka_spec = pl.BlockSpec((1, 1, seq, 2 * hd), lambda b, gg, i: (b, gg, 0, 0))
    vt_spec = pl.BlockSpec((1, 1, hd, seq), lambda b, gg, i: (b, gg, 0, 0))
    return pl.pallas_call(
        functools.partial(_attn_kernel, tq=tq, tk=tk, n_cmp=n_cmp, seq=seq),
        grid=(batch, g, seq // tq),
        in_specs=[pl.BlockSpec((1, tq, gw), lambda b, gg, i: (b, i, gg)),
                  pl.BlockSpec((1, 1, n_pad, hd), lambda b, gg, i: (b, gg, 0, 0)),
                  pl.BlockSpec((1, 1, n_pad, hd), lambda b, gg, i: (batch + b, gg, 0, 0)),
                  ka_spec, vt_spec, ka_spec, vt_spec,
                  pl.BlockSpec((1, tq, LANES), lambda b, gg, i: (b, i, gg)),
                  pl.BlockSpec((1, tq, gw), lambda b, gg, i: (b, i, gg))],
        out_specs=pl.BlockSpec((1, tq, gw), lambda b, gg, i: (b, i, gg)),
        out_shape=jax.ShapeDtypeStruct((batch, seq, N_HEADS * hd), BF16),
        scratch_shapes=[pltpu.VMEM((n_pad, 2 * hd), BF16), pltpu.VMEM((hd, n_pad), BF16)],
        compiler_params=_params("parallel", "parallel", "arbitrary"),
        name="nsa_attention",
    )(q, kvcmp, kvcmp, ksa, vst, kwa, vwt, gates, zs)


def _conv_kernel(cur_ref, halo_ref, w_ref, b_ref, g_ref, beta_ref, z_ref, o_ref, buf_ref, *, rc):
    i = pl.program_id(1)
    ts = cur_ref.shape[1]
    halo = halo_ref[0]
    buf_ref[0:CONV_HALO, :] = jnp.where(i > 0, halo, 0.0)
    buf_ref[CONV_HALO:CONV_HALO + ts, :] = cur_ref[0]
    first = CONV_HALO - (CONV_WIDTH - 1)
    for c in range(ts // rc):
        acc = jnp.zeros((rc, cur_ref.shape[2]), F32)
        for k in range(CONV_WIDTH):
            acc = acc + buf_ref[c * rc + first + k:c * rc + first + k + rc, :] * w_ref[k:k + 1, :]
        y = acc + b_ref[...]
        mu = jnp.mean(y, axis=-1, keepdims=True)
        d = y - mu
        var = jnp.mean(d * d, axis=-1, keepdims=True)
        y = d * lax.rsqrt(var + LN_EPS) * g_ref[...] + beta_ref[...]
        y = _silu(y) * z_ref[0, c * rc:(c + 1) * rc, :].astype(F32)
        o_ref[0, c * rc:(c + 1) * rc, :] = y.astype(o_ref.dtype)


def _conv_module(y_glu, zs, z_block, conv_w, conv_b, ln_g, ln_b, batch, seq, ts=256, rc=32):
    w = y_glu.shape[-1]
    ts = min(ts, seq)
    hb = ts // CONV_HALO
    vec = lambda a: a.reshape(1, w)
    vspec = pl.BlockSpec((1, w), lambda b, i: (0, 0))
    return pl.pallas_call(
        functools.partial(_conv_kernel, rc=rc),
        grid=(batch, seq // ts),
        in_specs=[pl.BlockSpec((1, ts, w), lambda b, i: (b, i, 0)),
                  pl.BlockSpec((1, CONV_HALO, w), lambda b, i: (b, jnp.maximum(i * hb - 1, 0), 0)),
                  pl.BlockSpec((CONV_WIDTH, w), lambda b, i: (0, 0)),
                  vspec, vspec, vspec,
                  pl.BlockSpec((1, ts, w), lambda b, i: (b, i, z_block))],
        out_specs=pl.BlockSpec((1, ts, w), lambda b, i: (b, i, 0)),
        out_shape=jax.ShapeDtypeStruct((batch, seq, w), BF16),
        scratch_shapes=[pltpu.VMEM((CONV_HALO + ts, w), F32)],
        compiler_params=_params("parallel", "parallel"),
        name="conv_module",
    )(y_glu, y_glu, conv_w, vec(conv_b), vec(ln_g), vec(ln_b), zs)


def _sgu_kernel(u_ref, v_ref, z_ref, g_ref, beta_ref, w_ref, bt_ref, o_ref):
    ts, width = v_ref.shape
    n_groups = w_ref.shape[0]
    gw = width // n_groups
    v = v_ref[...]
    mu = jnp.mean(v, axis=-1, keepdims=True)
    d = v - mu
    var = jnp.mean(d * d, axis=-1, keepdims=True)
    vn = (d * lax.rsqrt(var + LN_EPS) * g_ref[...] + beta_ref[...]).astype(BF16)
    tri = (lax.broadcasted_iota(jnp.int32, (SGU_CHUNK, SGU_CHUNK), 0)
           >= lax.broadcasted_iota(jnp.int32, (SGU_CHUNK, SGU_CHUNK), 1))
    for gi in range(n_groups):
        w_g = jnp.where(tri, w_ref[gi], 0.0).astype(BF16)
        cols = slice(gi * gw, (gi + 1) * gw)
        for c in range(ts // SGU_CHUNK):
            rws = slice(c * SGU_CHUNK, (c + 1) * SGU_CHUNK)
            s = _dot(w_g, vn[rws, cols]) + bt_ref[:, gi:gi + 1]
            out = u_ref[rws, cols].astype(F32) * s * z_ref[rws, cols].astype(F32)
            o_ref[rws, cols] = out.astype(o_ref.dtype)


def _sgu_module(u_src, u_block, v_src, v_block, zs, z_block, ln_g, ln_b, sgu_w, sgu_b, ts=512):
    t = v_src.shape[0]
    n_groups = sgu_w.shape[0]
    width = n_groups * sgu_w.shape[1]
    ts = min(ts, t)
    vspec = pl.BlockSpec((1, width), lambda i: (0, 0))
    return pl.pallas_call(
        _sgu_kernel,
        grid=(t // ts,),
        in_specs=[pl.BlockSpec((ts, width), lambda i: (i, u_block)),
                  pl.BlockSpec((ts, width), lambda i: (i, v_block)),
                  pl.BlockSpec((ts, width), lambda i: (i, z_block)),
                  vspec, vspec,
                  pl.BlockSpec(sgu_w.shape, lambda i: (0, 0, 0)),
                  pl.BlockSpec((SGU_CHUNK, n_groups), lambda i: (0, 0))],
        out_specs=pl.BlockSpec((ts, width), lambda i: (i, 0)),
        out_shape=jax.ShapeDtypeStruct((t, width), BF16),
        compiler_params=_params("parallel"),
        name="sgu_module",
    )(u_src, v_src, zs, ln_g.reshape(1, width), ln_b.reshape(1, width), sgu_w, sgu_b.T)


def _merge_kernel(h_ref, oa_ref, ob_ref, oc_ref, wm0_ref, wm1_ref, wm2_ref,
                  woa_ref, wob_ref, woc_ref, y_ref):
    h = h_ref[...]
    y = _sigmoid(_dot(h, wm0_ref[...])) * _dot(oa_ref[...], woa_ref[...])
    y = y + _sigmoid(_dot(h, wm1_ref[...])) * _dot(ob_ref[...], wob_ref[...])
    y = y + _sigmoid(_dot(h, wm2_ref[...])) * _dot(oc_ref[...], woc_ref[...])
    y_ref[...] = y.astype(y_ref.dtype)


def _merge(h, oa, ob, oc, w_merge, w_o_a, w_o_b, w_o_c, tm=512, tn=512):
    t, d = h.shape
    tm, tn = min(tm, t), min(tn, d)
    nb = d // tn

    def act(a):
        return pl.BlockSpec((tm, a.shape[1]), lambda j, i: (i, 0))

    def wm(branch):
        return pl.BlockSpec((d, tn), lambda j, i: (0, branch * nb + j))

    def wo(a):
        return pl.BlockSpec((a.shape[0], tn), lambda j, i: (0, j))

    return pl.pallas_call(
        _merge_kernel,
        grid=(nb, t // tm),
        in_specs=[act(h), act(oa), act(ob), act(oc), wm(0), wm(1), wm(2),
                  wo(w_o_a), wo(w_o_b), wo(w_o_c)],
        out_specs=pl.BlockSpec((tm, tn), lambda j, i: (i, j)),
        out_shape=jax.ShapeDtypeStruct((t, d), BF16),
        compiler_params=_params("parallel", "parallel"),
        name="merge",
    )(h, oa, ob, oc, w_merge, w_merge, w_merge, w_o_a, w_o_b, w_o_c)


def _out_kernel(y_ref, w_ref, x_ref, o_ref):
    o_ref[...] = x_ref[...] + _dot(y_ref[...], w_ref[...])


def _out_proj(y, w, x, tm=1024, tn=512):
    t, d = y.shape
    n = w.shape[1]
    tm, tn = min(tm, t), min(tn, n)
    return pl.pallas_call(
        _out_kernel,
        grid=(n // tn, t // tm),
        in_specs=[pl.BlockSpec((tm, d), lambda j, i: (i, 0)),
                  pl.BlockSpec((d, tn), lambda j, i: (0, j)),
                  pl.BlockSpec((tm, tn), lambda j, i: (i, j))],
        out_specs=pl.BlockSpec((tm, tn), lambda j, i: (i, j)),
        out_shape=jax.ShapeDtypeStruct((t, n), F32),
        compiler_params=_params("parallel", "parallel"),
        name="out_proj",
    )(y, w, x)


def _layer(x2d, batch, seq, norm_g, w_in, cmp_pos, cmp_w1, cmp_w2, w_o_a, conv_w, conv_b,
           conv_ln_g, conv_ln_b, w_o_b, sgu_ln_g, sgu_ln_b, sgu_w, sgu_b, w_o_c, w_out):
    d = x2d.shape[1]
    a_w = w_o_a.shape[0]
    b_w = w_o_b.shape[0]
    c_w = w_o_c.shape[0]
    kv_w = N_KV_GROUPS * HEAD_DIM
    n_gate = N_NSA_BRANCHES * N_HEADS
    hg = HEADS_PER_GROUP

    o_q = 0
    o_kv = o_q + a_w
    o_gate = o_kv + 6 * kv_w
    o_za = o_gate + n_gate
    o_b = o_za + a_w
    o_zb = o_b + 2 * b_w
    o_c = o_zb + b_w
    o_zc = o_c + 2 * c_w
    o_m = o_zc + c_w

    def cols(a, b):
        return w_in[:, a:b]

    w_q = cols(o_q, o_q + a_w).astype(BF16)
    w_bf = jnp.concatenate([cols(o_kv + 2 * kv_w, o_kv + 6 * kv_w), cols(o_c, o_c + c_w)],
                           axis=1).astype(BF16)
    w_f32 = jnp.concatenate([cols(o_kv, o_kv + 2 * kv_w), cols(o_c + c_w, o_c + 2 * c_w)],
                            axis=1).astype(BF16)
    w_silu = jnp.concatenate([cols(o_za, o_za + a_w), cols(o_zb, o_zb + b_w),
                              cols(o_zc, o_zc + c_w)], axis=1).astype(BF16)
    wg = cols(o_gate, o_gate + n_gate).reshape(d, N_NSA_BRANCHES, N_KV_GROUPS, hg)
    wg = wg.transpose(0, 2, 1, 3).reshape(d, N_KV_GROUPS, N_NSA_BRANCHES * hg)
    w_gate = jnp.pad(wg, ((0, 0), (0, 0), (0, LANES - N_NSA_BRANCHES * hg)))
    w_gate = w_gate.reshape(d, N_KV_GROUPS * LANES).astype(BF16)
    w_glu_u = cols(o_b, o_b + b_w).astype(BF16)
    w_glu_g = cols(o_b + b_w, o_b + 2 * b_w).astype(BF16)
    w_merge = cols(o_m, o_m + 3 * d).astype(BF16)

    h = _rmsnorm(x2d, norm_g, BF16)
    p_q = _proj(h, w_q, "scale", BF16, 1024, 1024, scale=ATTN_SCALE * LOG2E)
    p_bf = _proj(h, w_bf, "none", BF16, 1024, 1024)
    p_f32 = _proj(h, w_f32, "none", F32, 1024, 1024)
    p_silu = _proj(h, w_silu, "silu", BF16, 1024, 1024)
    p_gate = _proj(h, w_gate, "sigmoid", F32, 1024, 512)
    y_glu = _glu_proj(h, w_glu_u, w_glu_g, 1024, 512)

    def b3(a):
        return a.reshape(batch, seq, a.shape[-1])

    kvcmp = _compress(b3(p_f32), cmp_pos, cmp_w1, cmp_w2, batch, seq)
    o_a = _attention(b3(p_q), b3(p_bf), kvcmp, b3(p_gate), b3(p_silu), batch, seq)
    o_b_ = _conv_module(b3(y_glu), b3(p_silu), a_w // b_w, conv_w, conv_b, conv_ln_g, conv_ln_b,
                        batch, seq)
    o_c_ = _sgu_module(p_bf, 4 * kv_w // c_w, p_f32, 2 * kv_w // c_w,
                       p_silu, (a_w + b_w) // c_w, sgu_ln_g, sgu_ln_b, sgu_w, sgu_b)
    y = _merge(h, o_a.reshape(-1, a_w), o_b_.reshape(-1, b_w), o_c_, w_merge,
               w_o_a.astype(BF16), w_o_b.astype(BF16), w_o_c.astype(BF16))
    return _out_proj(y, w_out.astype(BF16), x2d)


def kernel(x, norm_g, w_in, cmp_pos, cmp_w1, cmp_w2, w_o_a, conv_w, conv_b, conv_ln_g, conv_ln_b,
           w_o_b, sgu_ln_g, sgu_ln_b, sgu_w, sgu_b, w_o_c, w_out, final_g):
    batch, seq, d = x.shape
    x2d = x.reshape(batch * seq, d)
    for l in range(norm_g.shape[0]):
        x2d = _layer(x2d, batch, seq, norm_g[l], w_in[l], cmp_pos[l], cmp_w1[l], cmp_w2[l],
                     w_o_a[l], conv_w[l], conv_b[l], conv_ln_g[l], conv_ln_b[l], w_o_b[l],
                     sgu_ln_g[l], sgu_ln_b[l], sgu_w[l], sgu_b[l], w_o_c[l], w_out[l])
    return _rmsnorm(x2d, final_g, x.dtype).reshape(batch, seq, d)
```

```python
import functools

import jax
import jax.numpy as jnp
from jax import lax
from jax.experimental import pallas as pl
from jax.experimental.pallas import tpu as pltpu

F32 = jnp.float32
BF16 = jnp.bfloat16

N_HEADS = 16
HEAD_DIM = 128
N_KV_GROUPS = 4
HEADS_PER_GROUP = N_HEADS // N_KV_GROUPS
N_NSA_BRANCHES = 3
CMP_BLOCK = 32
CMP_STRIDE = 16
SLC_BLOCK = 64
N_SELECT = 16
WINDOW = 512
ATTN_SCALE = HEAD_DIM ** -0.5
FORCE_SCORE = 1e9
NEG_INF = -1e30
CONV_WIDTH = 31
SGU_CHUNK = 128
RMS_EPS = 1e-6
LN_EPS = 1e-5

LANES = 128
SUBLANES = 8
CONV_HALO = 32
VMEM_LIMIT = 56 * 1024 * 1024


def _params(*semantics):
    return pltpu.CompilerParams(dimension_semantics=semantics, vmem_limit_bytes=VMEM_LIMIT)


def _dot(a, b):
    return jnp.dot(a, b, preferred_element_type=F32)


def _dot_nt(a, b):
    return lax.dot_general(a, b, (((1,), (1,)), ((), ())), preferred_element_type=F32)


def _sigmoid(x):
    return 1.0 / (1.0 + jnp.exp(-x))


def _silu(x):
    return x * _sigmoid(x)


def _rmsnorm_kernel(x_ref, g_ref, o_ref):
    x = x_ref[...]
    ms = jnp.mean(x * x, axis=-1, keepdims=True)
    o_ref[...] = (x * lax.rsqrt(ms + RMS_EPS) * g_ref[...]).astype(o_ref.dtype)


def _rmsnorm(x2d, g, out_dtype, tr=512):
    t, d = x2d.shape
    return pl.pallas_call(
        _rmsnorm_kernel,
        grid=(t // tr,),
        in_specs=[pl.BlockSpec((tr, d), lambda i: (i, 0)),
                  pl.BlockSpec((1, d), lambda i: (0, 0))],
        out_specs=pl.BlockSpec((tr, d), lambda i: (i, 0)),
        out_shape=jax.ShapeDtypeStruct((t, d), out_dtype),
        compiler_params=_params("parallel"),
        name="rmsnorm",
    )(x2d, g.reshape(1, d))


def _pack_kernel(w_ref, o_ref, *, slabs):
    dst = 0
    for src, width in slabs:
        o_ref[:, dst:dst + width] = w_ref[0, :, src:src + width].astype(o_ref.dtype)
        dst += width


def _pack_weights(w, layer, slabs, tr=64):
    _, k, n = w.shape
    total = sum(width for _, width in slabs)
    assert all(width % LANES == 0 for _, width in slabs)
    return pl.pallas_call(
        functools.partial(_pack_kernel, slabs=tuple(slabs)),
        grid=(k // tr,),
        in_specs=[pl.BlockSpec((1, tr, n), lambda i: (layer, i, 0))],
        out_specs=pl.BlockSpec((tr, total), lambda i: (i, 0)),
        out_shape=jax.ShapeDtypeStruct((k, total), BF16),
        compiler_params=_params("parallel"),
        name="pack_weights",
    )(w)


def _proj_kernel(x_ref, w_ref, o_ref, *, act, scale):
    acc = _dot(x_ref[...], w_ref[...])
    if act == "silu":
        acc = _silu(acc)
    elif act == "sigmoid":
        acc = _sigmoid(acc)
    elif act == "scale":
        acc = acc * scale
    o_ref[...] = acc.astype(o_ref.dtype)


def _proj(x, w, col, n, act, out_dtype, tm, tn, scale=None):
    t, k = x.shape
    tm, tn = min(tm, t), min(tn, n)
    assert col % tn == 0 and n % tn == 0
    cb = col // tn
    return pl.pallas_call(
        functools.partial(_proj_kernel, act=act, scale=scale),
        grid=(n // tn, t // tm),
        in_specs=[pl.BlockSpec((tm, k), lambda j, i: (i, 0)),
                  pl.BlockSpec((k, tn), lambda j, i: (0, cb + j))],
        out_specs=pl.BlockSpec((tm, tn), lambda j, i: (i, j)),
        out_shape=jax.ShapeDtypeStruct((t, n), out_dtype),
        compiler_params=_params("parallel", "parallel"),
        name="proj_" + act,
    )(x, w)


def _glu_kernel(x_ref, wu_ref, wg_ref, o_ref):
    x = x_ref[...]
    o_ref[...] = (_dot(x, wu_ref[...]) * _sigmoid(_dot(x, wg_ref[...]))).astype(o_ref.dtype)


def _glu_proj(x, w, col_u, col_g, n, tm, tn):
    t, k = x.shape
    tm, tn = min(tm, t), min(tn, n)
    assert col_u % tn == 0 and col_g % tn == 0 and n % tn == 0
    bu, bg = col_u // tn, col_g // tn
    return pl.pallas_call(
        _glu_kernel,
        grid=(n // tn, t // tm),
        in_specs=[pl.BlockSpec((tm, k), lambda j, i: (i, 0)),
                  pl.BlockSpec((k, tn), lambda j, i: (0, bu + j)),
                  pl.BlockSpec((k, tn), lambda j, i: (0, bg + j))],
        out_specs=pl.BlockSpec((tm, tn), lambda j, i: (i, j)),
        out_shape=jax.ShapeDtypeStruct((t, n), F32),
        compiler_params=_params("parallel", "parallel"),
        name="proj_glu",
    )(x, w, w)


def _compress_kernel(kv_ref, pos_ref, w1_ref, w2_ref, o_ref, *, n_cmp):
    n_pad = o_ref.shape[2]
    half = CMP_BLOCK // 2
    lo = jnp.zeros((n_pad, HEAD_DIM), F32)
    hi = jnp.zeros((n_pad, HEAD_DIM), F32)
    for l in range(half):
        rows = kv_ref[0, pl.ds(l, n_pad, stride=CMP_STRIDE), :]
        a = (rows + pos_ref[0, l:l + 1, :]).astype(BF16)
        b = (rows + pos_ref[0, half + l:half + l + 1, :]).astype(BF16)
        lo = lo + _dot(a, w1_ref[0, l * HEAD_DIM:(l + 1) * HEAD_DIM, :])
        hi = hi + _dot(b, w1_ref[0, (half + l) * HEAD_DIM:(half + l + 1) * HEAD_DIM, :])
    hid = lo + pltpu.roll(hi, n_pad - 1, 0)
    out = _dot(_silu(hid).astype(BF16), w2_ref[0])
    valid = lax.broadcasted_iota(jnp.int32, (n_pad, 1), 0) < n_cmp
    o_ref[0, 0] = jnp.where(valid, out, 0.0).astype(o_ref.dtype)


def _compress(kvc, cmp_pos, cmp_w1, cmp_w2, batch, seq):
    n_pad = seq // CMP_STRIDE
    n_cmp = (seq - CMP_BLOCK) // CMP_STRIDE + 1
    g = N_KV_GROUPS
    return pl.pallas_call(
        functools.partial(_compress_kernel, n_cmp=n_cmp),
        grid=(2, batch, g),
        in_specs=[pl.BlockSpec((1, seq, HEAD_DIM), lambda i, b, gg: (b, 0, i * g + gg)),
                  pl.BlockSpec((1, CMP_BLOCK, HEAD_DIM), lambda i, b, gg: (i, 0, 0)),
                  pl.BlockSpec((1, CMP_BLOCK * HEAD_DIM, HEAD_DIM), lambda i, b, gg: (i, 0, 0)),
                  pl.BlockSpec((1, HEAD_DIM, HEAD_DIM), lambda i, b, gg: (i, 0, 0))],
        out_specs=pl.BlockSpec((1, 1, n_pad, HEAD_DIM), lambda i, b, gg: (i * batch + b, gg, 0, 0)),
        out_shape=jax.ShapeDtypeStruct((2 * batch, g, n_pad, HEAD_DIM), BF16),
        compiler_params=_params("parallel", "parallel", "parallel"),
        name="compress",
    )(kvc, cmp_pos, cmp_w1.astype(BF16), cmp_w2.astype(BF16))


LOG2E = 1.4426950408889634
POS_COL = SLC_BLOCK
N_PIECES = 3


def _key_extra(pos):
    n = pos.shape[0]
    col = lax.broadcasted_iota(jnp.int32, (n, HEAD_DIM), 1)
    blk = pos >> (SLC_BLOCK.bit_length() - 1)
    hi = (blk * SLC_BLOCK).astype(F32)
    lo = (pos & (SLC_BLOCK - 1)).astype(F32)
    onehot = jnp.where(col == blk, 1.0, 0.0)
    return jnp.where(col < POS_COL, onehot,
                     jnp.where(col < POS_COL + N_PIECES, hi,
                               jnp.where(col < POS_COL + 2 * N_PIECES, lo, 0.0)))


def _kvprep_kernel(ks_ref, vs_ref, kw_ref, vw_ref, ksa_ref, vst_ref, kwa_ref, vwt_ref):
    ts = ks_ref.shape[1]
    pos = pl.program_id(2) * ts + lax.broadcasted_iota(jnp.int32, (ts, 1), 0)
    extra = _key_extra(pos).astype(BF16)
    for src, dst in ((ks_ref, ksa_ref), (kw_ref, kwa_ref)):
        dst[0, 0, :, 0:HEAD_DIM] = src[0]
        dst[0, 0, :, HEAD_DIM:2 * HEAD_DIM] = extra
    for src, dst in ((vs_ref, vst_ref), (vw_ref, vwt_ref)):
        dst[0, 0] = src[0].astype(F32).T.astype(BF16)


def _kv_prep(kv, batch, seq, ts=2048):
    ts = min(ts, seq)
    g = N_KV_GROUPS
    hd = HEAD_DIM
    ka = jax.ShapeDtypeStruct((batch, g, seq, 2 * hd), BF16)
    vt = jax.ShapeDtypeStruct((batch, g, hd, seq), BF16)
    ka_spec = pl.BlockSpec((1, 1, ts, 2 * hd), lambda b, gg, i: (b, gg, i, 0))
    vt_spec = pl.BlockSpec((1, 1, hd, ts), lambda b, gg, i: (b, gg, 0, i))

    def part(p):
        return pl.BlockSpec((1, ts, hd), lambda b, gg, i: (b, i, p * g + gg))

    return pl.pallas_call(
        _kvprep_kernel,
        grid=(batch, g, seq // ts),
        in_specs=[part(0), part(1), part(2), part(3)],
        out_specs=[ka_spec, vt_spec, ka_spec, vt_spec],
        out_shape=[ka, vt, ka, vt],
        compiler_params=_params("parallel", "parallel", "parallel"),
        name="kv_prep",
    )(kv, kv, kv, kv)


def _attn_kernel(q_ref, kc_ref, vc_ref, ksa_ref, vst_ref, kwa_ref, vwt_ref, gate_ref, z_ref, o_ref,
                 kca_ref, vct_ref, *, tq, tk, n_cmp, seq):
    hg = HEADS_PER_GROUP
    g = pl.program_id(1)
    i = pl.program_id(2)
    t0 = i * tq
    n_sblk = seq // SLC_BLOCK
    n_sel = min(N_SELECT, n_sblk)
    n_pad = kc_ref.shape[2]
    nrow = lax.broadcasted_iota(jnp.int32, (n_pad, 1), 0)
    end_c = nrow * CMP_STRIDE + (CMP_BLOCK - 1)

    @pl.when(i == 0)
    def _():
        kca_ref[:, 0:HEAD_DIM] = kc_ref[0, 0]
        kca_ref[:, HEAD_DIM:2 * HEAD_DIM] = _key_extra(end_c).astype(BF16)
        vct_ref[...] = vc_ref[0, 0].astype(F32).T.astype(BF16)

    t_q = t0 + lax.broadcasted_iota(jnp.int32, (1, tq), 1)
    lane = lax.broadcasted_iota(jnp.int32, (1, LANES), 1)
    kpos = lane - POS_COL

    def slope_cols(h):
        slope2 = jnp.exp2(jnp.full((1, LANES), hg * g + (h + 1), jnp.int32).astype(F32) * -0.5) * LOG2E
        piece0 = slope2.astype(BF16).astype(F32)
        rest = slope2 - piece0
        piece1 = rest.astype(BF16).astype(F32)
        piece2 = rest - piece1
        return jnp.where((kpos == 0) | (kpos == N_PIECES), piece0,
                         jnp.where((kpos == 1) | (kpos == N_PIECES + 1), piece1,
                                   jnp.where((kpos == 2) | (kpos == N_PIECES + 2), piece2, 0.0)))

    q_h = [q_ref[0, :, h * HEAD_DIM:(h + 1) * HEAD_DIM] for h in range(hg)]
    slopes = [slope_cols(h) for h in range(hg)]
    qa_plain = [jnp.concatenate([q_h[h], jnp.broadcast_to(slopes[h], (tq, LANES)).astype(BF16)], axis=1)
                for h in range(hg)]

    mask_c = (end_c <= t_q) & (nrow < n_cmp)
    kca = kca_ref[...]
    vct = vct_ref[...]
    o_cmp = []
    psum = jnp.zeros((n_pad, tq), F32)
    for h in range(hg):
        s = jnp.where(mask_c, _dot_nt(kca, qa_plain[h]), NEG_INF)
        m = jnp.max(s, axis=0, keepdims=True)
        p = jnp.exp2(s - m)
        l = jnp.sum(p, axis=0, keepdims=True)
        inv = jnp.where(m > 0.5 * NEG_INF, 1.0 / l, 0.0)
        o_cmp.append(_dot(vct, p.astype(BF16)) * inv)
        psum = psum + p * inv

    jrow = lax.broadcasted_iota(jnp.int32, (LANES, n_pad), 0)
    nn = lax.broadcasted_iota(jnp.int32, (LANES, n_pad), 1)
    ovl_t = ((nn * CMP_STRIDE < jrow * SLC_BLOCK + SLC_BLOCK)
             & (nn * CMP_STRIDE + CMP_BLOCK > jrow * SLC_BLOCK)
             & (nn < n_cmp) & (jrow < n_sblk)).astype(F32)
    imp_t = jnp.dot(ovl_t, psum, precision=lax.Precision.HIGHEST, preferred_element_type=F32)
    imp = imp_t[:n_sblk]
    blk = lax.broadcasted_iota(jnp.int32, (n_sblk, tq), 0)
    cur = (t0 + lax.broadcasted_iota(jnp.int32, (n_sblk, tq), 1)) >> (SLC_BLOCK.bit_length() - 1)
    forced = (blk == 0) | (blk == cur) | (blk == cur - 1)
    val = jnp.where(forced, FORCE_SCORE, jnp.where(blk <= cur, imp, NEG_INF))
    sub = 8
    vals = [val[k * sub:(k + 1) * sub] for k in range(n_sblk // sub)]
    ranks = [jnp.zeros((sub, tq), jnp.int32) for _ in vals]
    sub_row = lax.broadcasted_iota(jnp.int32, (sub, tq), 0)
    for jp in range(n_sblk):
        v_jp = jnp.broadcast_to(val[jp:jp + 1, :], (sub, tq))
        for k, v_k in enumerate(vals):
            if (k + 1) * sub <= jp:
                ahead = jnp.where(v_jp > v_k, 1, 0)
            elif k * sub > jp:
                ahead = jnp.where(v_jp >= v_k, 1, 0)
            else:
                ahead = jnp.where(sub_row + k * sub > jp,
                                  jnp.where(v_jp >= v_k, 1, 0), jnp.where(v_jp > v_k, 1, 0))
            ranks[k] = ranks[k] + ahead
    rank = jnp.concatenate(ranks, axis=0)
    selneg_t = jnp.where(rank < n_sel, 0.0, NEG_INF)
    if n_sblk < LANES:
        selneg_t = jnp.concatenate(
            [selneg_t, jnp.zeros((LANES - n_sblk, tq), F32)], axis=0)
    selneg = selneg_t.T
    qa_sel = [jnp.concatenate([q_h[h], jnp.where(lane < POS_COL, selneg, slopes[h]).astype(BF16)],
                              axis=1) for h in range(hg)]

    rel = (lax.broadcasted_iota(jnp.int32, (1, tq), 1)
           - lax.broadcasted_iota(jnp.int32, (tk, 1), 0))

    def run(qa, ka_ref, vt_ref, lo, hi, loop_mask, tail_masks):
        def tile(carry, j, mask_fn):
            p0 = pl.multiple_of(j * tk, tk)
            k_a = ka_ref[0, 0, pl.ds(p0, tk), :]
            v_t = vt_ref[0, 0, :, pl.ds(p0, tk)]
            mask = None if mask_fn is None else mask_fn(t0 - p0)
            sc = [_dot_nt(k_a, qa[h]) for h in range(hg)]
            probs, stats = [], []
            for h in range(hg):
                m_prev, l_prev, _ = carry[h]
                s = sc[h] if mask is None else jnp.where(mask, sc[h], NEG_INF)
                m_new = jnp.maximum(m_prev, jnp.max(s, axis=0, keepdims=True))
                alpha = jnp.exp2(m_prev - m_new)
                p = jnp.exp2(s - m_new)
                probs.append(p.astype(BF16))
                stats.append((m_new, alpha * l_prev + jnp.sum(p, axis=0, keepdims=True), alpha))
            return tuple((m_new, l_new, alpha * carry[h][2] + _dot(v_t, probs[h]))
                         for h, (m_new, l_new, alpha) in enumerate(stats))

        init = tuple((jnp.full((1, tq), -3e38, F32), jnp.zeros((1, tq), F32),
                      jnp.zeros((HEAD_DIM, tq), F32)) for _ in range(hg))
        carry = lax.fori_loop(lo, hi, lambda j, c: tile(c, j, loop_mask), init)
        for d, mask_fn in enumerate(tail_masks):
            carry = tile(carry, hi + d, mask_fn)
        return [acc * (1.0 / l) for _, l, acc in carry]

    def causal(off):
        return rel + off >= 0

    def window(off):
        dist = rel + off
        return (dist >= 0) & (dist < WINDOW)

    o_slc = run(qa_sel, ksa_ref, vst_ref, 0, t0 // tk, None, [causal] * max(tq // tk, 1))

    o_win = run(qa_plain, kwa_ref, vwt_ref, jnp.maximum(t0 - WINDOW + 1, 0) // tk,
                (t0 + tq - 1) // tk, window, [window])

    gate_t = gate_ref[0].T
    for h in range(hg):
        o_t = (gate_t[h:h + 1] * o_cmp[h] + gate_t[hg + h:hg + h + 1] * o_slc[h]
               + gate_t[2 * hg + h:2 * hg + h + 1] * o_win[h])
        zc = slice(h * HEAD_DIM, (h + 1) * HEAD_DIM)
        o_ref[0, :, zc] = (o_t.T * z_ref[0, :, zc].astype(F32)).astype(o_ref.dtype)


def _attention(q, kv, kvcmp, gates, zs, batch, seq, tq=512, tk=512):
    tq, tk = min(tq, seq), min(tk, seq)
    g = N_KV_GROUPS
    hd = HEAD_DIM
    gw = HEADS_PER_GROUP * hd
    n_pad = seq // CMP_STRIDE
    n_cmp = (seq - CMP_BLOCK) // CMP_STRIDE + 1
    assert seq // SLC_BLOCK <= POS_COL and (tq % tk == 0 or tk % tq == 0) and WINDOW % tk == 0
    ksa, vst, kwa, vwt = _kv_prep(kv, batch, seq)
    ka_spec = pl.BlockSpec((1, 1, seq, 2 * hd), lambda b, gg, i: (b, gg, 0, 0))
    vt_spec = pl.BlockSpec((1, 1, hd, seq), lambda b, gg, i: (b, gg, 0, 0))
    return pl.pallas_call(
        functools.partial(_attn_kernel, tq=tq, tk=tk, n_cmp=n_cmp, seq=seq),
        grid=(batch, g, seq // tq),
        in_specs=[pl.BlockSpec((1, tq, gw), lambda b, gg, i: (b, i, gg)),
                  pl.BlockSpec((1, 1, n_pad, hd), lambda b, gg, i: (b, gg, 0, 0)),
                  pl.BlockSpec((1, 1, n_pad, hd), lambda b, gg, i: (batch + b, gg, 0, 0)),
                  ka_spec, vt_spec, ka_spec, vt_spec,
                  pl.BlockSpec((1, tq, LANES), lambda b, gg, i: (b, i, gg)),
                  pl.BlockSpec((1, tq, gw), lambda b, gg, i: (b, i, gg))],
        out_specs=pl.BlockSpec((1, tq, gw), lambda b, gg, i: (b, i, gg)),
        out_shape=jax.ShapeDtypeStruct((batch, seq, N_HEADS * hd), BF16),
        scratch_shapes=[pltpu.VMEM((n_pad, 2 * hd), BF16), pltpu.VMEM((hd, n_pad), BF16)],
        compiler_params=_params("parallel", "parallel", "arbitrary"),
        name="nsa_attention",
    )(q, kvcmp, kvcmp, ksa, vst, kwa, vwt, gates, zs)


def _conv_kernel(cur_ref, halo_ref, w_ref, b_ref, g_ref, beta_ref, z_ref, o_ref, sh_ref, *, rc):
    i = pl.program_id(1)
    ts = cur_ref.shape[1]
    sub = sh_ref.shape[0]
    sh_ref[0, 0:CONV_HALO, :] = jnp.where(i > 0, halo_ref[0], 0.0)
    sh_ref[0, CONV_HALO:CONV_HALO + ts, :] = cur_ref[0]
    n_sh = CONV_HALO + ts - sub
    for s in range(1, sub):
        sh_ref[s, 0:n_sh, :] = sh_ref[0, s:s + n_sh, :]
    first = CONV_HALO - (CONV_WIDTH - 1)
    for c in range(ts // rc):
        acc = jnp.zeros((rc, cur_ref.shape[2]), F32)
        for k in range(CONV_WIDTH):
            s = (first + k) % sub
            base = c * rc + first + k - s
            acc = acc + sh_ref[s, base:base + rc, :] * w_ref[k:k + 1, :]
        y = acc + b_ref[...]
        mu = jnp.mean(y, axis=-1, keepdims=True)
        d = y - mu
        var = jnp.mean(d * d, axis=-1, keepdims=True)
        y = d * lax.rsqrt(var + LN_EPS) * g_ref[...] + beta_ref[...]
        y = _silu(y) * z_ref[0, c * rc:(c + 1) * rc, :].astype(F32)
        o_ref[0, c * rc:(c + 1) * rc, :] = y.astype(o_ref.dtype)


def _conv_module(y_glu, zs, z_block, conv_w, conv_b, ln_g, ln_b, batch, seq, ts=256, rc=32):
    w = y_glu.shape[-1]
    ts = min(ts, seq)
    hb = ts // CONV_HALO
    vec = lambda a: a.reshape(1, w)
    vspec = pl.BlockSpec((1, w), lambda b, i: (0, 0))
    return pl.pallas_call(
        functools.partial(_conv_kernel, rc=rc),
        grid=(batch, seq // ts),
        in_specs=[pl.BlockSpec((1, ts, w), lambda b, i: (b, i, 0)),
                  pl.BlockSpec((1, CONV_HALO, w), lambda b, i: (b, jnp.maximum(i * hb - 1, 0), 0)),
                  pl.BlockSpec((CONV_WIDTH, w), lambda b, i: (0, 0)),
                  vspec, vspec, vspec,
                  pl.BlockSpec((1, ts, w), lambda b, i: (b, i, z_block))],
        out_specs=pl.BlockSpec((1, ts, w), lambda b, i: (b, i, 0)),
        out_shape=jax.ShapeDtypeStruct((batch, seq, w), BF16),
        scratch_shapes=[pltpu.VMEM((SUBLANES, CONV_HALO + ts, w), F32)],
        compiler_params=_params("parallel", "parallel"),
        name="conv_module",
    )(y_glu, y_glu, conv_w, vec(conv_b), vec(ln_g), vec(ln_b), zs)


def _sgu_kernel(u_ref, v_ref, z_ref, g_ref, beta_ref, w_ref, bt_ref, o_ref):
    ts, width = v_ref.shape
    n_groups = w_ref.shape[0]
    gw = width // n_groups
    v = v_ref[...]
    mu = jnp.mean(v, axis=-1, keepdims=True)
    d = v - mu
    var = jnp.mean(d * d, axis=-1, keepdims=True)
    vn = (d * lax.rsqrt(var + LN_EPS) * g_ref[...] + beta_ref[...]).astype(BF16)
    tri = (lax.broadcasted_iota(jnp.int32, (SGU_CHUNK, SGU_CHUNK), 0)
           >= lax.broadcasted_iota(jnp.int32, (SGU_CHUNK, SGU_CHUNK), 1))
    for gi in range(n_groups):
        w_g = jnp.where(tri, w_ref[gi], 0.0).astype(BF16)
        cols = slice(gi * gw, (gi + 1) * gw)
        for c in range(ts // SGU_CHUNK):
            rws = slice(c * SGU_CHUNK, (c + 1) * SGU_CHUNK)
            s = _dot(w_g, vn[rws, cols]) + bt_ref[:, gi:gi + 1]
            out = u_ref[rws, cols].astype(F32) * s * z_ref[rws, cols].astype(F32)
            o_ref[rws, cols] = out.astype(o_ref.dtype)


def _sgu_module(u_src, u_block, v_src, v_block, zs, z_block, ln_g, ln_b, sgu_w, sgu_b, ts=512):
    t = v_src.shape[0]
    n_groups = sgu_w.shape[0]
    width = n_groups * sgu_w.shape[1]
    ts = min(ts, t)
    vspec = pl.BlockSpec((1, width), lambda i: (0, 0))
    return pl.pallas_call(
        _sgu_kernel,
        grid=(t // ts,),
        in_specs=[pl.BlockSpec((ts, width), lambda i: (i, u_block)),
                  pl.BlockSpec((ts, width), lambda i: (i, v_block)),
                  pl.BlockSpec((ts, width), lambda i: (i, z_block)),
                  vspec, vspec,
                  pl.BlockSpec(sgu_w.shape, lambda i: (0, 0, 0)),
                  pl.BlockSpec((SGU_CHUNK, n_groups), lambda i: (0, 0))],
        out_specs=pl.BlockSpec((ts, width), lambda i: (i, 0)),
        out_shape=jax.ShapeDtypeStruct((t, width), BF16),
        compiler_params=_params("parallel"),
        name="sgu_module",
    )(u_src, v_src, zs, ln_g.reshape(1, width), ln_b.reshape(1, width), sgu_w, sgu_b.T)


def _merge_kernel(h_ref, oa_ref, ob_ref, oc_ref, wm0_ref, wm1_ref, wm2_ref,
                  woa_ref, wob_ref, woc_ref, y_ref):
    h = h_ref[...]
    y = _sigmoid(_dot(h, wm0_ref[...])) * _dot(oa_ref[...], woa_ref[...])
    y = y + _sigmoid(_dot(h, wm1_ref[...])) * _dot(ob_ref[...], wob_ref[...])
    y = y + _sigmoid(_dot(h, wm2_ref[...])) * _dot(oc_ref[...], woc_ref[...])
    y_ref[...] = y.astype(y_ref.dtype)


def _merge(h, oa, ob, oc, w_merge, col, w_o_a, w_o_b, w_o_c, tm=512, tn=512):
    t, d = h.shape
    tm, tn = min(tm, t), min(tn, d)
    assert col % tn == 0
    nb = d // tn
    cb = col // tn

    def act(a):
        return pl.BlockSpec((tm, a.shape[1]), lambda j, i: (i, 0))

    def wm(branch):
        return pl.BlockSpec((d, tn), lambda j, i: (0, cb + branch * nb + j))

    def wo(a):
        return pl.BlockSpec((a.shape[0], tn), lambda j, i: (0, j))

    return pl.pallas_call(
        _merge_kernel,
        grid=(nb, t // tm),
        in_specs=[act(h), act(oa), act(ob), act(oc), wm(0), wm(1), wm(2),
                  wo(w_o_a), wo(w_o_b), wo(w_o_c)],
        out_specs=pl.BlockSpec((tm, tn), lambda j, i: (i, j)),
        out_shape=jax.ShapeDtypeStruct((t, d), BF16),
        compiler_params=_params("parallel", "parallel"),
        name="merge",
    )(h, oa, ob, oc, w_merge, w_merge, w_merge, w_o_a, w_o_b, w_o_c)


def _out_kernel(y_ref, w_ref, x_ref, o_ref):
    o_ref[...] = x_ref[...] + _dot(y_ref[...], w_ref[...])


def _out_proj(y, w, x, tm=1024, tn=512):
    t, d = y.shape
    n = w.shape[1]
    tm, tn = min(tm, t), min(tn, n)
    return pl.pallas_call(
        _out_kernel,
        grid=(n // tn, t // tm),
        in_specs=[pl.BlockSpec((tm, d), lambda j, i: (i, 0)),
                  pl.BlockSpec((d, tn), lambda j, i: (0, j)),
                  pl.BlockSpec((tm, tn), lambda j, i: (i, j))],
        out_specs=pl.BlockSpec((tm, tn), lambda j, i: (i, j)),
        out_shape=jax.ShapeDtypeStruct((t, n), F32),
        compiler_params=_params("parallel", "parallel"),
        name="out_proj",
    )(y, w, x)


def _layer(x2d, batch, seq, layer, norm_g, w_in, cmp_pos, cmp_w1, cmp_w2, w_o_a, conv_w, conv_b,
           conv_ln_g, conv_ln_b, w_o_b, sgu_ln_g, sgu_ln_b, sgu_w, sgu_b, w_o_c, w_out):
    d = x2d.shape[1]
    a_w = w_o_a.shape[0]
    b_w = w_o_b.shape[0]
    c_w = w_o_c.shape[0]
    kv_w = N_KV_GROUPS * HEAD_DIM
    n_gate = N_NSA_BRANCHES * N_HEADS
    hg = HEADS_PER_GROUP

    o_q = 0
    o_kv = o_q + a_w
    o_gate = o_kv + 6 * kv_w
    o_za = o_gate + n_gate
    o_b = o_za + a_w
    o_zb = o_b + 2 * b_w
    o_c = o_zb + b_w
    o_zc = o_c + 2 * c_w
    o_m = o_zc + c_w

    slabs = [(o_q, a_w),
             (o_kv + 2 * kv_w, 4 * kv_w), (o_c, c_w),
             (o_kv, 2 * kv_w), (o_c + c_w, c_w),
             (o_za, a_w), (o_zb, b_w), (o_zc, c_w),
             (o_b, b_w), (o_b + b_w, b_w),
             (o_m, 3 * d)]
    wp = _pack_weights(w_in, layer, slabs)
    p_q0 = 0
    p_bf0 = p_q0 + a_w
    p_f0 = p_bf0 + 4 * kv_w + c_w
    p_s0 = p_f0 + 2 * kv_w + c_w
    p_glu0 = p_s0 + a_w + b_w + c_w
    p_m0 = p_glu0 + 2 * b_w
    wg = w_in[layer, :, o_gate:o_gate + n_gate].reshape(d, N_NSA_BRANCHES, N_KV_GROUPS, hg)
    wg = wg.transpose(0, 2, 1, 3).reshape(d, N_KV_GROUPS, N_NSA_BRANCHES * hg)
    w_gate = jnp.pad(wg, ((0, 0), (0, 0), (0, LANES - N_NSA_BRANCHES * hg)))
    w_gate = w_gate.reshape(d, N_KV_GROUPS * LANES).astype(BF16)

    h = _rmsnorm(x2d, norm_g, BF16)
    p_q = _proj(h, wp, p_q0, a_w, "scale", BF16, 1024, 1024, scale=ATTN_SCALE * LOG2E)
    p_bf = _proj(h, wp, p_bf0, 4 * kv_w + c_w, "none", BF16, 1024, 1024)
    p_f32 = _proj(h, wp, p_f0, 2 * kv_w + c_w, "none", F32, 1024, 1024)
    p_silu = _proj(h, wp, p_s0, a_w + b_w + c_w, "silu", BF16, 1024, 1024)
    p_gate = _proj(h, w_gate, 0, N_KV_GROUPS * LANES, "sigmoid", F32, 1024, 512)
    y_glu = _glu_proj(h, wp, p_glu0, p_glu0 + b_w, b_w, 1024, 512)

    def b3(a):
        return a.reshape(batch, seq, a.shape[-1])

    kvcmp = _compress(b3(p_f32), cmp_pos, cmp_w1, cmp_w2, batch, seq)
    o_a = _attention(b3(p_q), b3(p_bf), kvcmp, b3(p_gate), b3(p_silu), batch, seq)
    o_b_ = _conv_module(b3(y_glu), b3(p_silu), a_w // b_w, conv_w, conv_b, conv_ln_g, conv_ln_b,
                        batch, seq)
    o_c_ = _sgu_module(p_bf, 4 * kv_w // c_w, p_f32, 2 * kv_w // c_w,
                       p_silu, (a_w + b_w) // c_w, sgu_ln_g, sgu_ln_b, sgu_w, sgu_b)
    y = _merge(h, o_a.reshape(-1, a_w), o_b_.reshape(-1, b_w), o_c_, wp, p_m0,
               w_o_a.astype(BF16), w_o_b.astype(BF16), w_o_c.astype(BF16))
    return _out_proj(y, w_out.astype(BF16), x2d)


def kernel(x, norm_g, w_in, cmp_pos, cmp_w1, cmp_w2, w_o_a, conv_w, conv_b, conv_ln_g, conv_ln_b,
           w_o_b, sgu_ln_g, sgu_ln_b, sgu_w, sgu_b, w_o_c, w_out, final_g):
    batch, seq, d = x.shape
    x2d = x.reshape(batch * seq, d)
    for l in range(norm_g.shape[0]):
        x2d = _layer(x2d, batch, seq, l, norm_g[l], w_in, cmp_pos[l], cmp_w1[l], cmp_w2[l],
                     w_o_a[l], conv_w[l], conv_b[l], conv_ln_g[l], conv_ln_b[l], w_o_b[l],
                     sgu_ln_g[l], sgu_ln_b[l], sgu_w[l], sgu_b[l], w_o_c[l], w_out[l])
    return _rmsnorm(x2d, final_g, x.dtype).reshape(batch, seq, d)
```

```python
import functools

import jax
import jax.numpy as jnp
from jax import lax
from jax.experimental import pallas as pl
from jax.experimental.pallas import tpu as pltpu

F32 = jnp.float32
BF16 = jnp.bfloat16

N_HEADS = 16
HEAD_DIM = 128
N_KV_GROUPS = 4
HEADS_PER_GROUP = N_HEADS // N_KV_GROUPS
N_NSA_BRANCHES = 3
CMP_BLOCK = 32
CMP_STRIDE = 16
SLC_BLOCK = 64
N_SELECT = 16
WINDOW = 512
ATTN_SCALE = HEAD_DIM ** -0.5
FORCE_SCORE = 1e9
NEG_INF = -1e30
CONV_WIDTH = 31
SGU_CHUNK = 128
RMS_EPS = 1e-6
LN_EPS = 1e-5

LANES = 128
SUBLANES = 8
CONV_HALO = 32
VMEM_LIMIT = 56 * 1024 * 1024


def _params(*semantics):
    return pltpu.CompilerParams(dimension_semantics=semantics, vmem_limit_bytes=VMEM_LIMIT)


def _dot(a, b):
    return jnp.dot(a, b, preferred_element_type=F32)


def _dot_nt(a, b):
    return lax.dot_general(a, b, (((1,), (1,)), ((), ())), preferred_element_type=F32)


def _sigmoid(x):
    return 1.0 / (1.0 + jnp.exp(-x))


def _silu(x):
    return x * _sigmoid(x)


def _rmsnorm_kernel(x_ref, g_ref, o_ref):
    x = x_ref[...]
    ms = jnp.mean(x * x, axis=-1, keepdims=True)
    o_ref[...] = (x * lax.rsqrt(ms + RMS_EPS) * g_ref[...]).astype(o_ref.dtype)


def _rmsnorm(x2d, g, out_dtype, tr=512):
    t, d = x2d.shape
    return pl.pallas_call(
        _rmsnorm_kernel,
        grid=(t // tr,),
        in_specs=[pl.BlockSpec((tr, d), lambda i: (i, 0)),
                  pl.BlockSpec((1, d), lambda i: (0, 0))],
        out_specs=pl.BlockSpec((tr, d), lambda i: (i, 0)),
        out_shape=jax.ShapeDtypeStruct((t, d), out_dtype),
        compiler_params=_params("parallel"),
        name="rmsnorm",
    )(x2d, g.reshape(1, d))


def _proj_kernel(x_ref, w_ref, o_ref, *, act, scale):
    acc = _dot_nt(x_ref[...], w_ref[...])
    if act == "silu":
        acc = _silu(acc)
    elif act == "sigmoid":
        acc = _sigmoid(acc)
    elif act == "scale":
        acc = acc * scale
    o_ref[...] = acc.astype(o_ref.dtype)


def _pick(j, values):
    r = values[0]
    for idx, v in enumerate(values[1:], 1):
        r = jnp.where(j == idx, v, r)
    return r


ROW_ALIGN = 16


def _row_window(rows, k, start_fn):
    return pl.BlockSpec((pl.Element(rows), pl.Element(k)),
                        lambda *ids: (pl.multiple_of(start_fn(*ids), ROW_ALIGN), 0))


def _tile_starts(slabs, tn):
    assert all(width % tn == 0 for _, width in slabs)
    return tuple(start + r for start, width in slabs for r in range(0, width, tn))


def _proj(x, w_t, starts, act, out_dtype, tm, tn, scale=None):
    t, k = x.shape
    tm = min(tm, t)
    return pl.pallas_call(
        functools.partial(_proj_kernel, act=act, scale=scale),
        grid=(len(starts), t // tm),
        in_specs=[pl.BlockSpec((tm, k), lambda j, i: (i, 0)),
                  _row_window(tn, k, lambda j, i: _pick(j, starts))],
        out_specs=pl.BlockSpec((tm, tn), lambda j, i: (i, j)),
        out_shape=jax.ShapeDtypeStruct((t, len(starts) * tn), out_dtype),
        compiler_params=_params("parallel", "parallel"),
        name="proj_" + act,
    )(x, w_t)


def _glu_kernel(x_ref, wu_ref, wg_ref, o_ref):
    x = x_ref[...]
    o_ref[...] = (_dot_nt(x, wu_ref[...]) * _sigmoid(_dot_nt(x, wg_ref[...]))).astype(o_ref.dtype)


def _glu_proj(x, w_t, row_u, row_g, n, tm, tn):
    t, k = x.shape
    tm, tn = min(tm, t), min(tn, n)
    assert n % tn == 0
    return pl.pallas_call(
        _glu_kernel,
        grid=(n // tn, t // tm),
        in_specs=[pl.BlockSpec((tm, k), lambda j, i: (i, 0)),
                  _row_window(tn, k, lambda j, i: row_u + j * tn),
                  _row_window(tn, k, lambda j, i: row_g + j * tn)],
        out_specs=pl.BlockSpec((tm, tn), lambda j, i: (i, j)),
        out_shape=jax.ShapeDtypeStruct((t, n), F32),
        compiler_params=_params("parallel", "parallel"),
        name="proj_glu",
    )(x, w_t, w_t)


def _compress_kernel(kv_ref, pos_ref, w1_ref, w2_ref, o_ref, *, n_cmp):
    n_pad = o_ref.shape[2]
    half = CMP_BLOCK // 2
    lo = jnp.zeros((n_pad, HEAD_DIM), F32)
    hi = jnp.zeros((n_pad, HEAD_DIM), F32)
    for l in range(half):
        rows = kv_ref[0, pl.ds(l, n_pad, stride=CMP_STRIDE), :]
        a = (rows + pos_ref[0, l:l + 1, :]).astype(BF16)
        b = (rows + pos_ref[0, half + l:half + l + 1, :]).astype(BF16)
        lo = lo + _dot(a, w1_ref[0, l * HEAD_DIM:(l + 1) * HEAD_DIM, :])
        hi = hi + _dot(b, w1_ref[0, (half + l) * HEAD_DIM:(half + l + 1) * HEAD_DIM, :])
    hid = lo + pltpu.roll(hi, n_pad - 1, 0)
    out = _dot(_silu(hid).astype(BF16), w2_ref[0])
    valid = lax.broadcasted_iota(jnp.int32, (n_pad, 1), 0) < n_cmp
    o_ref[0, 0] = jnp.where(valid, out, 0.0).astype(o_ref.dtype)


def _compress(kvc, cmp_pos, cmp_w1, cmp_w2, batch, seq):
    n_pad = seq // CMP_STRIDE
    n_cmp = (seq - CMP_BLOCK) // CMP_STRIDE + 1
    g = N_KV_GROUPS
    return pl.pallas_call(
        functools.partial(_compress_kernel, n_cmp=n_cmp),
        grid=(2, batch, g),
        in_specs=[pl.BlockSpec((1, seq, HEAD_DIM), lambda i, b, gg: (b, 0, i * g + gg)),
                  pl.BlockSpec((1, CMP_BLOCK, HEAD_DIM), lambda i, b, gg: (i, 0, 0)),
                  pl.BlockSpec((1, CMP_BLOCK * HEAD_DIM, HEAD_DIM), lambda i, b, gg: (i, 0, 0)),
                  pl.BlockSpec((1, HEAD_DIM, HEAD_DIM), lambda i, b, gg: (i, 0, 0))],
        out_specs=pl.BlockSpec((1, 1, n_pad, HEAD_DIM), lambda i, b, gg: (i * batch + b, gg, 0, 0)),
        out_shape=jax.ShapeDtypeStruct((2 * batch, g, n_pad, HEAD_DIM), BF16),
        compiler_params=_params("parallel", "parallel", "parallel"),
        name="compress",
    )(kvc, cmp_pos, cmp_w1.astype(BF16), cmp_w2.astype(BF16))


LOG2E = 1.4426950408889634
POS_COL = SLC_BLOCK
N_PIECES = 3


def _key_extra(pos):
    n = pos.shape[0]
    col = lax.broadcasted_iota(jnp.int32, (n, HEAD_DIM), 1)
    blk = pos >> (SLC_BLOCK.bit_length() - 1)
    hi = (blk * SLC_BLOCK).astype(F32)
    lo = (pos & (SLC_BLOCK - 1)).astype(F32)
    onehot = jnp.where(col == blk, 1.0, 0.0)
    return jnp.where(col < POS_COL, onehot,
                     jnp.where(col < POS_COL + N_PIECES, hi,
                               jnp.where(col < POS_COL + 2 * N_PIECES, lo, 0.0)))


def _kvprep_kernel(ks_ref, vs_ref, kw_ref, vw_ref, ksa_ref, vst_ref, kwa_ref, vwt_ref):
    ts = ks_ref.shape[1]
    pos = pl.program_id(2) * ts + lax.broadcasted_iota(jnp.int32, (ts, 1), 0)
    extra = _key_extra(pos).astype(BF16)
    for src, dst in ((ks_ref, ksa_ref), (kw_ref, kwa_ref)):
        dst[0, 0, :, 0:HEAD_DIM] = src[0]
        dst[0, 0, :, HEAD_DIM:2 * HEAD_DIM] = extra
    for src, dst in ((vs_ref, vst_ref), (vw_ref, vwt_ref)):
        dst[0, 0] = src[0].astype(F32).T.astype(BF16)


def _kv_prep(kv, batch, seq, ts=2048):
    ts = min(ts, seq)
    g = N_KV_GROUPS
    hd = HEAD_DIM
    ka = jax.ShapeDtypeStruct((batch, g, seq, 2 * hd), BF16)
    vt = jax.ShapeDtypeStruct((batch, g, hd, seq), BF16)
    ka_spec = pl.BlockSpec((1, 1, ts, 2 * hd), lambda b, gg, i: (b, gg, i, 0))
    vt_spec = pl.BlockSpec((1, 1, hd, ts), lambda b, gg, i: (b, gg, 0, i))

    def part(p):
        return pl.BlockSpec((1, ts, hd), lambda b, gg, i: (b, i, p * g + gg))

    return pl.pallas_call(
        _kvprep_kernel,
        grid=(batch, g, seq // ts),
        in_specs=[part(0), part(1), part(2), part(3)],
        out_specs=[ka_spec, vt_spec, ka_spec, vt_spec],
        out_shape=[ka, vt, ka, vt],
        compiler_params=_params("parallel", "parallel", "parallel"),
        name="kv_prep",
    )(kv, kv, kv, kv)


def _attn_kernel(q_ref, kc_ref, vc_ref, ksa_ref, vst_ref, kwa_ref, vwt_ref, gate_ref, z_ref, o_ref,
                 kca_ref, vct_ref, *, tq, tk, n_cmp, seq):
    hg = HEADS_PER_GROUP
    g = pl.program_id(1)
    i = pl.program_id(2)
    t0 = i * tq
    n_sblk = seq // SLC_BLOCK
    n_sel = min(N_SELECT, n_sblk)
    n_pad = kc_ref.shape[2]
    nrow = lax.broadcasted_iota(jnp.int32, (n_pad, 1), 0)
    end_c = nrow * CMP_STRIDE + (CMP_BLOCK - 1)

    @pl.when(i == 0)
    def _():
        kca_ref[:, 0:HEAD_DIM] = kc_ref[0, 0]
        kca_ref[:, HEAD_DIM:2 * HEAD_DIM] = _key_extra(end_c).astype(BF16)
        vct_ref[...] = vc_ref[0, 0].astype(F32).T.astype(BF16)

    t_q = t0 + lax.broadcasted_iota(jnp.int32, (1, tq), 1)
    lane = lax.broadcasted_iota(jnp.int32, (1, LANES), 1)
    kpos = lane - POS_COL

    def slope_cols(h):
        slope2 = jnp.exp2(jnp.full((1, LANES), hg * g + (h + 1), jnp.int32).astype(F32) * -0.5) * LOG2E
        piece0 = slope2.astype(BF16).astype(F32)
        rest = slope2 - piece0
        piece1 = rest.astype(BF16).astype(F32)
        piece2 = rest - piece1
        return jnp.where((kpos == 0) | (kpos == N_PIECES), piece0,
                         jnp.where((kpos == 1) | (kpos == N_PIECES + 1), piece1,
                                   jnp.where((kpos == 2) | (kpos == N_PIECES + 2), piece2, 0.0)))

    q_h = [q_ref[0, :, h * HEAD_DIM:(h + 1) * HEAD_DIM] for h in range(hg)]
    slopes = [slope_cols(h) for h in range(hg)]
    qa_plain = [jnp.concatenate([q_h[h], jnp.broadcast_to(slopes[h], (tq, LANES)).astype(BF16)], axis=1)
                for h in range(hg)]

    mask_c = (end_c <= t_q) & (nrow < n_cmp)
    kca = kca_ref[...]
    vct = vct_ref[...]
    o_cmp = []
    psum = jnp.zeros((n_pad, tq), F32)
    for h in range(hg):
        s = jnp.where(mask_c, _dot_nt(kca, qa_plain[h]), NEG_INF)
        m = jnp.max(s, axis=0, keepdims=True)
        p = jnp.exp2(s - m)
        l = jnp.sum(p, axis=0, keepdims=True)
        inv = jnp.where(m > 0.5 * NEG_INF, 1.0 / l, 0.0)
        o_cmp.append(_dot(vct, p.astype(BF16)) * inv)
        psum = psum + p * inv

    jrow = lax.broadcasted_iota(jnp.int32, (LANES, n_pad), 0)
    nn = lax.broadcasted_iota(jnp.int32, (LANES, n_pad), 1)
    ovl_t = ((nn * CMP_STRIDE < jrow * SLC_BLOCK + SLC_BLOCK)
             & (nn * CMP_STRIDE + CMP_BLOCK > jrow * SLC_BLOCK)
             & (nn < n_cmp) & (jrow < n_sblk)).astype(F32)
    imp_t = jnp.dot(ovl_t, psum, precision=lax.Precision.HIGHEST, preferred_element_type=F32)
    imp = imp_t[:n_sblk]
    blk = lax.broadcasted_iota(jnp.int32, (n_sblk, tq), 0)
    cur = (t0 + lax.broadcasted_iota(jnp.int32, (n_sblk, tq), 1)) >> (SLC_BLOCK.bit_length() - 1)
    forced = (blk == 0) | (blk == cur) | (blk == cur - 1)
    val = jnp.where(forced, FORCE_SCORE, jnp.where(blk <= cur, imp, NEG_INF))
    sub = 8
    vals = [val[k * sub:(k + 1) * sub] for k in range(n_sblk // sub)]
    ranks = [jnp.zeros((sub, tq), jnp.int32) for _ in vals]
    sub_row = lax.broadcasted_iota(jnp.int32, (sub, tq), 0)
    for jp in range(n_sblk):
        v_jp = jnp.broadcast_to(val[jp:jp + 1, :], (sub, tq))
        for k, v_k in enumerate(vals):
            if (k + 1) * sub <= jp:
                ahead = jnp.where(v_jp > v_k, 1, 0)
            elif k * sub > jp:
                ahead = jnp.where(v_jp >= v_k, 1, 0)
            else:
                ahead = jnp.where(sub_row + k * sub > jp,
                                  jnp.where(v_jp >= v_k, 1, 0), jnp.where(v_jp > v_k, 1, 0))
            ranks[k] = ranks[k] + ahead
    rank = jnp.concatenate(ranks, axis=0)
    selneg_t = jnp.where(rank < n_sel, 0.0, NEG_INF)
    if n_sblk < LANES:
        selneg_t = jnp.concatenate(
            [selneg_t, jnp.zeros((LANES - n_sblk, tq), F32)], axis=0)
    selneg = selneg_t.T
    qa_sel = [jnp.concatenate([q_h[h], jnp.where(lane < POS_COL, selneg, slopes[h]).astype(BF16)],
                              axis=1) for h in range(hg)]

    rel = (lax.broadcasted_iota(jnp.int32, (1, tq), 1)
           - lax.broadcasted_iota(jnp.int32, (tk, 1), 0))

    def run(qa, ka_ref, vt_ref, lo, hi, loop_mask, tail_masks):
        def tile(carry, j, mask_fn):
            p0 = pl.multiple_of(j * tk, tk)
            k_a = ka_ref[0, 0, pl.ds(p0, tk), :]
            v_t = vt_ref[0, 0, :, pl.ds(p0, tk)]
            mask = None if mask_fn is None else mask_fn(t0 - p0)
            sc = [_dot_nt(k_a, qa[h]) for h in range(hg)]
            probs, stats = [], []
            for h in range(hg):
                m_prev, l_prev, _ = carry[h]
                s = sc[h] if mask is None else jnp.where(mask, sc[h], NEG_INF)
                m_new = jnp.maximum(m_prev, jnp.max(s, axis=0, keepdims=True))
                alpha = jnp.exp2(m_prev - m_new)
                p = jnp.exp2(s - m_new)
                probs.append(p.astype(BF16))
                stats.append((m_new, alpha * l_prev + jnp.sum(p, axis=0, keepdims=True), alpha))
            return tuple((m_new, l_new, alpha * carry[h][2] + _dot(v_t, probs[h]))
                         for h, (m_new, l_new, alpha) in enumerate(stats))

        init = tuple((jnp.full((1, tq), -3e38, F32), jnp.zeros((1, tq), F32),
                      jnp.zeros((HEAD_DIM, tq), F32)) for _ in range(hg))
        carry = lax.fori_loop(lo, hi, lambda j, c: tile(c, j, loop_mask), init)
        for d, mask_fn in enumerate(tail_masks):
            carry = tile(carry, hi + d, mask_fn)
        return [acc * (1.0 / l) for _, l, acc in carry]

    def causal(off):
        return rel + off >= 0

    def window(off):
        dist = rel + off
        return (dist >= 0) & (dist < WINDOW)

    o_slc = run(qa_sel, ksa_ref, vst_ref, 0, t0 // tk, None, [causal] * max(tq // tk, 1))

    o_win = run(qa_plain, kwa_ref, vwt_ref, jnp.maximum(t0 - WINDOW + 1, 0) // tk,
                (t0 + tq - 1) // tk, window, [window])

    gate_t = gate_ref[0].T
    for h in range(hg):
        o_t = (gate_t[h:h + 1] * o_cmp[h] + gate_t[hg + h:hg + h + 1] * o_slc[h]
               + gate_t[2 * hg + h:2 * hg + h + 1] * o_win[h])
        zc = slice(h * HEAD_DIM, (h + 1) * HEAD_DIM)
        o_ref[0, :, zc] = (o_t.T * z_ref[0, :, zc].astype(F32)).astype(o_ref.dtype)


def _attention(q, kv, kvcmp, gates, zs, batch, seq, tq=512, tk=512):
    tq, tk = min(tq, seq), min(tk, seq)
    g = N_KV_GROUPS
    hd = HEAD_DIM
    gw = HEADS_PER_GROUP * hd
    n_pad = seq // CMP_STRIDE
    n_cmp = (seq - CMP_BLOCK) // CMP_STRIDE + 1
    assert seq // SLC_BLOCK <= POS_COL and (tq % tk == 0 or tk % tq == 0) and WINDOW % tk == 0
    ksa, vst, kwa, vwt = _kv_prep(kv, batch, seq)
    ka_spec = pl.BlockSpec((1, 1, seq, 2 * hd), lambda b, gg, i: (b, gg, 0, 0))
    vt_spec = pl.BlockSpec((1, 1, hd, seq), lambda b, gg, i: (b, gg, 0, 0))
    return pl.pallas_call(
        functools.partial(_attn_kernel, tq=tq, tk=tk, n_cmp=n_cmp, seq=seq),
        grid=(batch, g, seq // tq),
        in_specs=[pl.BlockSpec((1, tq, gw), lambda b, gg, i: (b, i, gg)),
                  pl.BlockSpec((1, 1, n_pad, hd), lambda b, gg, i: (b, gg, 0, 0)),
                  pl.BlockSpec((1, 1, n_pad, hd), lambda b, gg, i: (batch + b, gg, 0, 0)),
                  ka_spec, vt_spec, ka_spec, vt_spec,
                  pl.BlockSpec((1, tq, LANES), lambda b, gg, i: (b, i, gg)),
                  pl.BlockSpec((1, tq, gw), lambda b, gg, i: (b, i, gg))],
        out_specs=pl.BlockSpec((1, tq, gw), lambda b, gg, i: (b, i, gg)),
        out_shape=jax.ShapeDtypeStruct((batch, seq, N_HEADS * hd), BF16),
        scratch_shapes=[pltpu.VMEM((n_pad, 2 * hd), BF16), pltpu.VMEM((hd, n_pad), BF16)],
        compiler_params=_params("parallel", "parallel", "arbitrary"),
        name="nsa_attention",
    )(q, kvcmp, kvcmp, ksa, vst, kwa, vwt, gates, zs)


def _conv_kernel(cur_ref, halo_ref, w_ref, b_ref, g_ref, beta_ref, z_ref, o_ref, sh_ref, *, rc):
    i = pl.program_id(1)
    ts = cur_ref.shape[1]
    sub = sh_ref.shape[0]
    sh_ref[0, 0:CONV_HALO, :] = jnp.where(i > 0, halo_ref[0], 0.0)
    sh_ref[0, CONV_HALO:CONV_HALO + ts, :] = cur_ref[0]
    n_sh = CONV_HALO + ts - sub
    for s in range(1, sub):
        sh_ref[s, 0:n_sh, :] = sh_ref[0, s:s + n_sh, :]
    first = CONV_HALO - (CONV_WIDTH - 1)
    for c in range(ts // rc):
        acc = jnp.zeros((rc, cur_ref.shape[2]), F32)
        for k in range(CONV_WIDTH):
            s = (first + k) % sub
            base = c * rc + first + k - s
            acc = acc + sh_ref[s, base:base + rc, :] * w_ref[k:k + 1, :]
        y = acc + b_ref[...]
        mu = jnp.mean(y, axis=-1, keepdims=True)
        d = y - mu
        var = jnp.mean(d * d, axis=-1, keepdims=True)
        y = d * lax.rsqrt(var + LN_EPS) * g_ref[...] + beta_ref[...]
        y = _silu(y) * z_ref[0, c * rc:(c + 1) * rc, :].astype(F32)
        o_ref[0, c * rc:(c + 1) * rc, :] = y.astype(o_ref.dtype)


def _conv_module(y_glu, zs, z_block, conv_w, conv_b, ln_g, ln_b, batch, seq, ts=256, rc=32):
    w = y_glu.shape[-1]
    ts = min(ts, seq)
    hb = ts // CONV_HALO
    vec = lambda a: a.reshape(1, w)
    vspec = pl.BlockSpec((1, w), lambda b, i: (0, 0))
    return pl.pallas_call(
        functools.partial(_conv_kernel, rc=rc),
        grid=(batch, seq // ts),
        in_specs=[pl.BlockSpec((1, ts, w), lambda b, i: (b, i, 0)),
                  pl.BlockSpec((1, CONV_HALO, w), lambda b, i: (b, jnp.maximum(i * hb - 1, 0), 0)),
                  pl.BlockSpec((CONV_WIDTH, w), lambda b, i: (0, 0)),
                  vspec, vspec, vspec,
                  pl.BlockSpec((1, ts, w), lambda b, i: (b, i, z_block))],
        out_specs=pl.BlockSpec((1, ts, w), lambda b, i: (b, i, 0)),
        out_shape=jax.ShapeDtypeStruct((batch, seq, w), BF16),
        scratch_shapes=[pltpu.VMEM((SUBLANES, CONV_HALO + ts, w), F32)],
        compiler_params=_params("parallel", "parallel"),
        name="conv_module",
    )(y_glu, y_glu, conv_w, vec(conv_b), vec(ln_g), vec(ln_b), zs)


def _sgu_kernel(u_ref, v_ref, z_ref, g_ref, beta_ref, w_ref, bt_ref, o_ref):
    ts, width = v_ref.shape
    n_groups = w_ref.shape[0]
    gw = width // n_groups
    v = v_ref[...]
    mu = jnp.mean(v, axis=-1, keepdims=True)
    d = v - mu
    var = jnp.mean(d * d, axis=-1, keepdims=True)
    vn = (d * lax.rsqrt(var + LN_EPS) * g_ref[...] + beta_ref[...]).astype(BF16)
    tri = (lax.broadcasted_iota(jnp.int32, (SGU_CHUNK, SGU_CHUNK), 0)
           >= lax.broadcasted_iota(jnp.int32, (SGU_CHUNK, SGU_CHUNK), 1))
    for gi in range(n_groups):
        w_g = jnp.where(tri, w_ref[gi], 0.0).astype(BF16)
        cols = slice(gi * gw, (gi + 1) * gw)
        for c in range(ts // SGU_CHUNK):
            rws = slice(c * SGU_CHUNK, (c + 1) * SGU_CHUNK)
            s = _dot(w_g, vn[rws, cols]) + bt_ref[:, gi:gi + 1]
            out = u_ref[rws, cols].astype(F32) * s * z_ref[rws, cols].astype(F32)
            o_ref[rws, cols] = out.astype(o_ref.dtype)


def _sgu_module(u_src, u_block, v_src, v_block, zs, z_block, ln_g, ln_b, sgu_w, sgu_b, ts=512):
    t = v_src.shape[0]
    n_groups = sgu_w.shape[0]
    width = n_groups * sgu_w.shape[1]
    ts = min(ts, t)
    vspec = pl.BlockSpec((1, width), lambda i: (0, 0))
    return pl.pallas_call(
        _sgu_kernel,
        grid=(t // ts,),
        in_specs=[pl.BlockSpec((ts, width), lambda i: (i, u_block)),
                  pl.BlockSpec((ts, width), lambda i: (i, v_block)),
                  pl.BlockSpec((ts, width), lambda i: (i, z_block)),
                  vspec, vspec,
                  pl.BlockSpec(sgu_w.shape, lambda i: (0, 0, 0)),
                  pl.BlockSpec((SGU_CHUNK, n_groups), lambda i: (0, 0))],
        out_specs=pl.BlockSpec((ts, width), lambda i: (i, 0)),
        out_shape=jax.ShapeDtypeStruct((t, width), BF16),
        compiler_params=_params("parallel"),
        name="sgu_module",
    )(u_src, v_src, zs, ln_g.reshape(1, width), ln_b.reshape(1, width), sgu_w, sgu_b.T)


def _merge_kernel(h_ref, oa_ref, ob_ref, oc_ref, wm0_ref, wm1_ref, wm2_ref,
                  woa_ref, wob_ref, woc_ref, y_ref):
    h = h_ref[...]
    y = _sigmoid(_dot_nt(h, wm0_ref[...])) * _dot(oa_ref[...], woa_ref[...])
    y = y + _sigmoid(_dot_nt(h, wm1_ref[...])) * _dot(ob_ref[...], wob_ref[...])
    y = y + _sigmoid(_dot_nt(h, wm2_ref[...])) * _dot(oc_ref[...], woc_ref[...])
    y_ref[...] = y.astype(y_ref.dtype)


def _merge(h, oa, ob, oc, w_t, row, w_o_a, w_o_b, w_o_c, tm=512, tn=512):
    t, d = h.shape
    tm, tn = min(tm, t), min(tn, d)
    nb = d // tn

    def act(a):
        return pl.BlockSpec((tm, a.shape[1]), lambda j, i: (i, 0))

    def wm(branch):
        return _row_window(tn, d, lambda j, i: row + branch * d + j * tn)

    def wo(a):
        return pl.BlockSpec((a.shape[0], tn), lambda j, i: (0, j))

    return pl.pallas_call(
        _merge_kernel,
        grid=(nb, t // tm),
        in_specs=[act(h), act(oa), act(ob), act(oc), wm(0), wm(1), wm(2),
                  wo(w_o_a), wo(w_o_b), wo(w_o_c)],
        out_specs=pl.BlockSpec((tm, tn), lambda j, i: (i, j)),
        out_shape=jax.ShapeDtypeStruct((t, d), BF16),
        compiler_params=_params("parallel", "parallel"),
        name="merge",
    )(h, oa, ob, oc, w_t, w_t, w_t, w_o_a, w_o_b, w_o_c)


def _out_kernel(y_ref, w_ref, x_ref, o_ref):
    o_ref[...] = x_ref[...] + _dot(y_ref[...], w_ref[...])


def _out_proj(y, w, x, tm=1024, tn=512):
    t, d = y.shape
    n = w.shape[1]
    tm, tn = min(tm, t), min(tn, n)
    return pl.pallas_call(
        _out_kernel,
        grid=(t // tm, n // tn),
        in_specs=[pl.BlockSpec((tm, d), lambda i, j: (i, 0)),
                  pl.BlockSpec((d, tn), lambda i, j: (0, j)),
                  pl.BlockSpec((tm, tn), lambda i, j: (i, j))],
        out_specs=pl.BlockSpec((tm, tn), lambda i, j: (i, j)),
        out_shape=jax.ShapeDtypeStruct((t, n), F32),
        compiler_params=_params("parallel", "parallel"),
        name="out_proj",
    )(y, w, x)


def _layer(x2d, batch, seq, layer, norm_g, w_in, cmp_pos, cmp_w1, cmp_w2, w_o_a, conv_w, conv_b,
           conv_ln_g, conv_ln_b, w_o_b, sgu_ln_g, sgu_ln_b, sgu_w, sgu_b, w_o_c, w_out):
    d = x2d.shape[1]
    a_w = w_o_a.shape[0]
    b_w = w_o_b.shape[0]
    c_w = w_o_c.shape[0]
    kv_w = N_KV_GROUPS * HEAD_DIM
    n_gate = N_NSA_BRANCHES * N_HEADS
    hg = HEADS_PER_GROUP

    o_q = 0
    o_kv = o_q + a_w
    o_gate = o_kv + 6 * kv_w
    o_za = o_gate + n_gate
    o_b = o_za + a_w
    o_zb = o_b + 2 * b_w
    o_c = o_zb + b_w
    o_zc = o_c + 2 * c_w
    o_m = o_zc + c_w

    w_t = jnp.swapaxes(w_in, 1, 2)[layer]
    wp = w_t.astype(BF16)
    wg = w_t[o_gate:o_gate + n_gate].reshape(N_NSA_BRANCHES, N_KV_GROUPS, hg, d)
    wg = wg.transpose(1, 0, 2, 3).reshape(N_KV_GROUPS, N_NSA_BRANCHES * hg, d)
    wg = jnp.pad(wg, ((0, 0), (0, LANES - N_NSA_BRANCHES * hg), (0, 0)))
    wg = wg.reshape(N_KV_GROUPS * LANES, d).astype(BF16)

    tn = 1024
    h = _rmsnorm(x2d, norm_g, BF16)
    p_q = _proj(h, wp, _tile_starts([(o_q, a_w)], tn), "scale", BF16, 1024, tn, scale=ATTN_SCALE * LOG2E)
    p_bf = _proj(h, wp, _tile_starts([(o_kv + 2 * kv_w, 4 * kv_w), (o_c, c_w)], tn),
                 "none", BF16, 1024, tn)
    p_f32 = _proj(h, wp, _tile_starts([(o_kv, 2 * kv_w), (o_c + c_w, c_w)], tn),
                  "none", F32, 1024, tn)
    p_silu = _proj(h, wp, _tile_starts([(o_za, a_w), (o_zb, b_w), (o_zc, c_w)], tn),
                   "silu", BF16, 1024, tn)
    p_gate = _proj(h, wg, (0,), "sigmoid", F32, 1024, N_KV_GROUPS * LANES)
    y_glu = _glu_proj(h, wp, o_b, o_b + b_w, b_w, 1024, 512)

    def b3(a):
        return a.reshape(batch, seq, a.shape[-1])

    kvcmp = _compress(b3(p_f32), cmp_pos, cmp_w1, cmp_w2, batch, seq)
    o_a = _attention(b3(p_q), b3(p_bf), kvcmp, b3(p_gate), b3(p_silu), batch, seq)
    o_b_ = _conv_module(b3(y_glu), b3(p_silu), a_w // b_w, conv_w, conv_b, conv_ln_g, conv_ln_b,
                        batch, seq)
    o_c_ = _sgu_module(p_bf, 4 * kv_w // c_w, p_f32, 2 * kv_w // c_w,
                       p_silu, (a_w + b_w) // c_w, sgu_ln_g, sgu_ln_b, sgu_w, sgu_b)
    y = _merge(h, o_a.reshape(-1, a_w), o_b_.reshape(-1, b_w), o_c_, wp, o_m,
               w_o_a.astype(BF16), w_o_b.astype(BF16), w_o_c.astype(BF16))
    return _out_proj(y, w_out.astype(BF16), x2d)


def kernel(x, norm_g, w_in, cmp_pos, cmp_w1, cmp_w2, w_o_a, conv_w, conv_b, conv_ln_g, conv_ln_b,
           w_o_b, sgu_ln_g, sgu_ln_b, sgu_w, sgu_b, w_o_c, w_out, final_g):
    batch, seq, d = x.shape
    x2d = x.reshape(batch * seq, d)
    for l in range(norm_g.shape[0]):
        x2d = _layer(x2d, batch, seq, l, norm_g[l], w_in, cmp_pos[l], cmp_w1[l], cmp_w2[l],
                     w_o_a[l], conv_w[l], conv_b[l], conv_ln_g[l], conv_ln_b[l], w_o_b[l],
                     sgu_ln_g[l], sgu_ln_b[l], sgu_w[l], sgu_b[l], w_o_c[l], w_out[l])
    return _rmsnorm(x2d, final_g, x.dtype).reshape(batch, seq, d)
```

```python
import functools

import jax
import jax.numpy as jnp
from jax import lax
from jax.experimental import pallas as pl
from jax.experimental.pallas import tpu as pltpu

F32 = jnp.float32
BF16 = jnp.bfloat16

N_HEADS = 16
HEAD_DIM = 128
N_KV_GROUPS = 4
HEADS_PER_GROUP = N_HEADS // N_KV_GROUPS
N_NSA_BRANCHES = 3
CMP_BLOCK = 32
CMP_STRIDE = 16
SLC_BLOCK = 64
N_SELECT = 16
WINDOW = 512
ATTN_SCALE = HEAD_DIM ** -0.5
FORCE_SCORE = 1e9
NEG_INF = -1e30
CONV_WIDTH = 31
SGU_CHUNK = 128
RMS_EPS = 1e-6
LN_EPS = 1e-5

LANES = 128
SUBLANES = 8
CONV_HALO = 32
VMEM_LIMIT = 56 * 1024 * 1024


def _params(*semantics):
    return pltpu.CompilerParams(dimension_semantics=semantics, vmem_limit_bytes=VMEM_LIMIT)


def _dot(a, b):
    return jnp.dot(a, b, preferred_element_type=F32)


def _dot_nt(a, b):
    return lax.dot_general(a, b, (((1,), (1,)), ((), ())), preferred_element_type=F32)


def _sigmoid(x):
    return 1.0 / (1.0 + jnp.exp(-x))


def _silu(x):
    return x * _sigmoid(x)


def _rmsnorm_kernel(x_ref, g_ref, o_ref):
    x = x_ref[...]
    ms = jnp.mean(x * x, axis=-1, keepdims=True)
    o_ref[...] = (x * lax.rsqrt(ms + RMS_EPS) * g_ref[...]).astype(o_ref.dtype)


def _rmsnorm(x2d, g, out_dtype, tr=512):
    t, d = x2d.shape
    return pl.pallas_call(
        _rmsnorm_kernel,
        grid=(t // tr,),
        in_specs=[pl.BlockSpec((tr, d), lambda i: (i, 0)),
                  pl.BlockSpec((1, d), lambda i: (0, 0))],
        out_specs=pl.BlockSpec((tr, d), lambda i: (i, 0)),
        out_shape=jax.ShapeDtypeStruct((t, d), out_dtype),
        compiler_params=_params("parallel"),
        name="rmsnorm",
    )(x2d, g.reshape(1, d))


def _proj_kernel(x_ref, w_ref, o_ref, *, act, scale):
    acc = _dot_nt(x_ref[...], w_ref[...])
    if act == "silu":
        acc = _silu(acc)
    elif act == "sigmoid":
        acc = _sigmoid(acc)
    elif act == "scale":
        acc = acc * scale
    o_ref[...] = acc.astype(o_ref.dtype)


def _pick(j, values):
    r = values[0]
    for idx, v in enumerate(values[1:], 1):
        r = jnp.where(j == idx, v, r)
    return r


ROW_ALIGN = 16


def _row_window(rows, k, start_fn):
    return pl.BlockSpec((pl.Element(rows), pl.Element(k)),
                        lambda *ids: (pl.multiple_of(start_fn(*ids), ROW_ALIGN), 0))


def _tile_starts(slabs, tn):
    assert all(width % tn == 0 for _, width in slabs)
    return tuple(start + r for start, width in slabs for r in range(0, width, tn))


def _proj(x, w_t, starts, act, out_dtype, tm, tn, scale=None):
    t, k = x.shape
    tm = min(tm, t)
    return pl.pallas_call(
        functools.partial(_proj_kernel, act=act, scale=scale),
        grid=(len(starts), t // tm),
        in_specs=[pl.BlockSpec((tm, k), lambda j, i: (i, 0)),
                  _row_window(tn, k, lambda j, i: _pick(j, starts))],
        out_specs=pl.BlockSpec((tm, tn), lambda j, i: (i, j)),
        out_shape=jax.ShapeDtypeStruct((t, len(starts) * tn), out_dtype),
        compiler_params=_params("parallel", "parallel"),
        name="proj_" + act,
    )(x, w_t)


def _glu_kernel(x_ref, wu_ref, wg_ref, o_ref):
    x = x_ref[...]
    o_ref[...] = (_dot_nt(x, wu_ref[...]) * _sigmoid(_dot_nt(x, wg_ref[...]))).astype(o_ref.dtype)


def _glu_proj(x, w_t, row_u, row_g, n, tm, tn):
    t, k = x.shape
    tm, tn = min(tm, t), min(tn, n)
    assert n % tn == 0
    return pl.pallas_call(
        _glu_kernel,
        grid=(n // tn, t // tm),
        in_specs=[pl.BlockSpec((tm, k), lambda j, i: (i, 0)),
                  _row_window(tn, k, lambda j, i: row_u + j * tn),
                  _row_window(tn, k, lambda j, i: row_g + j * tn)],
        out_specs=pl.BlockSpec((tm, tn), lambda j, i: (i, j)),
        out_shape=jax.ShapeDtypeStruct((t, n), F32),
        compiler_params=_params("parallel", "parallel"),
        name="proj_glu",
    )(x, w_t, w_t)


def _compress_kernel(kv_ref, pos_ref, w1_ref, w2_ref, o_ref, *, n_cmp):
    n_pad = o_ref.shape[2]
    half = CMP_BLOCK // 2
    lo = jnp.zeros((n_pad, HEAD_DIM), F32)
    hi = jnp.zeros((n_pad, HEAD_DIM), F32)
    for l in range(half):
        rows = kv_ref[0, pl.ds(l, n_pad, stride=CMP_STRIDE), :]
        a = (rows + pos_ref[0, l:l + 1, :]).astype(BF16)
        b = (rows + pos_ref[0, half + l:half + l + 1, :]).astype(BF16)
        lo = lo + _dot(a, w1_ref[0, l * HEAD_DIM:(l + 1) * HEAD_DIM, :])
        hi = hi + _dot(b, w1_ref[0, (half + l) * HEAD_DIM:(half + l + 1) * HEAD_DIM, :])
    hid = lo + pltpu.roll(hi, n_pad - 1, 0)
    out = _dot(_silu(hid).astype(BF16), w2_ref[0])
    valid = lax.broadcasted_iota(jnp.int32, (n_pad, 1), 0) < n_cmp
    o_ref[0, 0] = jnp.where(valid, out, 0.0).astype(o_ref.dtype)


def _compress(kvc, cmp_pos, cmp_w1, cmp_w2, batch, seq):
    n_pad = seq // CMP_STRIDE
    n_cmp = (seq - CMP_BLOCK) // CMP_STRIDE + 1
    g = N_KV_GROUPS
    return pl.pallas_call(
        functools.partial(_compress_kernel, n_cmp=n_cmp),
        grid=(2, batch, g),
        in_specs=[pl.BlockSpec((1, seq, HEAD_DIM), lambda i, b, gg: (b, 0, i * g + gg)),
                  pl.BlockSpec((1, CMP_BLOCK, HEAD_DIM), lambda i, b, gg: (i, 0, 0)),
                  pl.BlockSpec((1, CMP_BLOCK * HEAD_DIM, HEAD_DIM), lambda i, b, gg: (i, 0, 0)),
                  pl.BlockSpec((1, HEAD_DIM, HEAD_DIM), lambda i, b, gg: (i, 0, 0))],
        out_specs=pl.BlockSpec((1, 1, n_pad, HEAD_DIM), lambda i, b, gg: (i * batch + b, gg, 0, 0)),
        out_shape=jax.ShapeDtypeStruct((2 * batch, g, n_pad, HEAD_DIM), BF16),
        compiler_params=_params("parallel", "parallel", "parallel"),
        name="compress",
    )(kvc, cmp_pos, cmp_w1.astype(BF16), cmp_w2.astype(BF16))


LOG2E = 1.4426950408889634
POS_COL = SLC_BLOCK
N_PIECES = 3
V_ROWS = HEAD_DIM + ROW_ALIGN


def _key_extra(pos):
    n = pos.shape[0]
    col = lax.broadcasted_iota(jnp.int32, (n, HEAD_DIM), 1)
    blk = pos >> (SLC_BLOCK.bit_length() - 1)
    hi = (blk * SLC_BLOCK).astype(F32)
    lo = (pos & (SLC_BLOCK - 1)).astype(F32)
    onehot = jnp.where(col == blk, 1.0, 0.0)
    return jnp.where(col < POS_COL, onehot,
                     jnp.where(col < POS_COL + N_PIECES, hi,
                               jnp.where(col < POS_COL + 2 * N_PIECES, lo, 0.0)))


def _kvprep_kernel(ks_ref, vs_ref, kw_ref, vw_ref, ksa_ref, vst_ref, kwa_ref, vwt_ref):
    ts = ks_ref.shape[1]
    pos = pl.program_id(2) * ts + lax.broadcasted_iota(jnp.int32, (ts, 1), 0)
    extra = _key_extra(pos).astype(BF16)
    for src, dst in ((ks_ref, ksa_ref), (kw_ref, kwa_ref)):
        dst[0, 0, :, 0:HEAD_DIM] = src[0]
        dst[0, 0, :, HEAD_DIM:2 * HEAD_DIM] = extra
    ones_rows = jnp.where(lax.broadcasted_iota(jnp.int32, (V_ROWS - HEAD_DIM, ts), 0) == 0, 1.0, 0.0)
    for src, dst in ((vs_ref, vst_ref), (vw_ref, vwt_ref)):
        dst[0, 0, 0:HEAD_DIM, :] = src[0].astype(F32).T.astype(BF16)
        dst[0, 0, HEAD_DIM:V_ROWS, :] = ones_rows.astype(BF16)


def _kv_prep(kv, batch, seq, ts=2048):
    ts = min(ts, seq)
    g = N_KV_GROUPS
    hd = HEAD_DIM
    ka = jax.ShapeDtypeStruct((batch, g, seq, 2 * hd), BF16)
    vt = jax.ShapeDtypeStruct((batch, g, V_ROWS, seq), BF16)
    ka_spec = pl.BlockSpec((1, 1, ts, 2 * hd), lambda b, gg, i: (b, gg, i, 0))
    vt_spec = pl.BlockSpec((1, 1, V_ROWS, ts), lambda b, gg, i: (b, gg, 0, i))

    def part(p):
        return pl.BlockSpec((1, ts, hd), lambda b, gg, i: (b, i, p * g + gg))

    return pl.pallas_call(
        _kvprep_kernel,
        grid=(batch, g, seq // ts),
        in_specs=[part(0), part(1), part(2), part(3)],
        out_specs=[ka_spec, vt_spec, ka_spec, vt_spec],
        out_shape=[ka, vt, ka, vt],
        compiler_params=_params("parallel", "parallel", "parallel"),
        name="kv_prep",
    )(kv, kv, kv, kv)


def _attn_kernel(q_ref, kc_ref, vc_ref, ksa_ref, vst_ref, kwa_ref, vwt_ref, gate_ref, z_ref, o_ref,
                 kca_ref, vct_ref, *, tq, tk, n_cmp, seq):
    hg = HEADS_PER_GROUP
    g = pl.program_id(1)
    i = pl.program_id(2)
    t0 = i * tq
    n_sblk = seq // SLC_BLOCK
    n_sel = min(N_SELECT, n_sblk)
    n_pad = kc_ref.shape[2]
    nrow = lax.broadcasted_iota(jnp.int32, (n_pad, 1), 0)
    end_c = nrow * CMP_STRIDE + (CMP_BLOCK - 1)

    @pl.when(i == 0)
    def _():
        kca_ref[:, 0:HEAD_DIM] = kc_ref[0, 0]
        kca_ref[:, HEAD_DIM:2 * HEAD_DIM] = _key_extra(end_c).astype(BF16)
        vct_ref[...] = vc_ref[0, 0].astype(F32).T.astype(BF16)

    t_q = t0 + lax.broadcasted_iota(jnp.int32, (1, tq), 1)
    lane = lax.broadcasted_iota(jnp.int32, (1, LANES), 1)
    kpos = lane - POS_COL

    def slope_cols(h):
        slope2 = jnp.exp2(jnp.full((1, LANES), hg * g + (h + 1), jnp.int32).astype(F32) * -0.5) * LOG2E
        piece0 = slope2.astype(BF16).astype(F32)
        rest = slope2 - piece0
        piece1 = rest.astype(BF16).astype(F32)
        piece2 = rest - piece1
        return jnp.where((kpos == 0) | (kpos == N_PIECES), piece0,
                         jnp.where((kpos == 1) | (kpos == N_PIECES + 1), piece1,
                                   jnp.where((kpos == 2) | (kpos == N_PIECES + 2), piece2, 0.0)))

    q_h = [q_ref[0, :, h * HEAD_DIM:(h + 1) * HEAD_DIM] for h in range(hg)]
    slopes = [slope_cols(h) for h in range(hg)]
    qa_plain = [jnp.concatenate([q_h[h], jnp.broadcast_to(slopes[h], (tq, LANES)).astype(BF16)], axis=1)
                for h in range(hg)]

    mask_c = (end_c <= t_q) & (nrow < n_cmp)
    kca = kca_ref[...]
    vct = vct_ref[...]
    o_cmp = []
    psum = jnp.zeros((n_pad, tq), F32)
    for h in range(hg):
        s = jnp.where(mask_c, _dot_nt(kca, qa_plain[h]), NEG_INF)
        m = jnp.max(s, axis=0, keepdims=True)
        p = jnp.exp2(s - m)
        l = jnp.sum(p, axis=0, keepdims=True)
        inv = jnp.where(m > 0.5 * NEG_INF, 1.0 / l, 0.0)
        o_cmp.append(_dot(vct, p.astype(BF16)) * inv)
        psum = psum + p * inv

    jrow = lax.broadcasted_iota(jnp.int32, (LANES, n_pad), 0)
    nn = lax.broadcasted_iota(jnp.int32, (LANES, n_pad), 1)
    ovl_t = ((nn * CMP_STRIDE < jrow * SLC_BLOCK + SLC_BLOCK)
             & (nn * CMP_STRIDE + CMP_BLOCK > jrow * SLC_BLOCK)
             & (nn < n_cmp) & (jrow < n_sblk)).astype(F32)
    imp_t = jnp.dot(ovl_t, psum, precision=lax.Precision.HIGHEST, preferred_element_type=F32)
    imp = imp_t[:n_sblk]
    blk = lax.broadcasted_iota(jnp.int32, (n_sblk, tq), 0)
    cur = (t0 + lax.broadcasted_iota(jnp.int32, (n_sblk, tq), 1)) >> (SLC_BLOCK.bit_length() - 1)
    forced = (blk == 0) | (blk == cur) | (blk == cur - 1)
    val = jnp.where(forced, FORCE_SCORE, jnp.where(blk <= cur, imp, NEG_INF))
    sub = 8
    vals = [val[k * sub:(k + 1) * sub] for k in range(n_sblk // sub)]
    ranks = [jnp.zeros((sub, tq), jnp.int32) for _ in vals]
    sub_row = lax.broadcasted_iota(jnp.int32, (sub, tq), 0)
    for jp in range(n_sblk):
        v_jp = jnp.broadcast_to(val[jp:jp + 1, :], (sub, tq))
        for k, v_k in enumerate(vals):
            if (k + 1) * sub <= jp:
                ahead = jnp.where(v_jp > v_k, 1, 0)
            elif k * sub > jp:
                ahead = jnp.where(v_jp >= v_k, 1, 0)
            else:
                ahead = jnp.where(sub_row + k * sub > jp,
                                  jnp.where(v_jp >= v_k, 1, 0), jnp.where(v_jp > v_k, 1, 0))
            ranks[k] = ranks[k] + ahead
    rank = jnp.concatenate(ranks, axis=0)
    selneg_t = jnp.where(rank < n_sel, 0.0, NEG_INF)
    if n_sblk < LANES:
        selneg_t = jnp.concatenate(
            [selneg_t, jnp.zeros((LANES - n_sblk, tq), F32)], axis=0)
    selneg = selneg_t.T
    qa_sel = [jnp.concatenate([q_h[h], jnp.where(lane < POS_COL, selneg, slopes[h]).astype(BF16)],
                              axis=1) for h in range(hg)]

    rel = (lax.broadcasted_iota(jnp.int32, (1, tq), 1)
           - lax.broadcasted_iota(jnp.int32, (tk, 1), 0))

    whole = ((0, tk, 0, tq),)

    def run(qa, ka_ref, vt_ref, lo, hi, loop_mask, tail_masks):
        def block(carry, p0, mask_fn, part):
            k_lo, k_hi, q_lo, q_hi = part
            k_a = ka_ref[0, 0, pl.ds(p0 + k_lo, k_hi - k_lo), :]
            v_t = vt_ref[0, 0, :, pl.ds(p0 + k_lo, k_hi - k_lo)]
            mask = None if mask_fn is None else mask_fn(rel[k_lo:k_hi, q_lo:q_hi] + (t0 - p0))
            sc = [_dot_nt(k_a, qa[h][q_lo:q_hi]) for h in range(hg)]
            probs, stats = [], []
            for h in range(hg):
                m_prev = carry[h][0][:, q_lo:q_hi]
                s = sc[h] if mask is None else jnp.where(mask, sc[h], NEG_INF)
                m_new = jnp.maximum(m_prev, jnp.max(s, axis=0, keepdims=True))
                probs.append(jnp.exp2((s - m_new).astype(BF16)))
                stats.append((m_new, jnp.exp2(m_prev - m_new)))
            new = []
            for h, (m_new, alpha) in enumerate(stats):
                acc = alpha * carry[h][1][:, q_lo:q_hi] + _dot(v_t, probs[h])
                if (q_lo, q_hi) != (0, tq):
                    m_new, acc = (
                        jnp.concatenate(([old[:, :q_lo]] if q_lo else []) + [part_new]
                                        + ([old[:, q_hi:]] if q_hi < tq else []), axis=1)
                        for old, part_new in zip(carry[h], (m_new, acc)))
                new.append((m_new, acc))
            return tuple(new)

        def tile(carry, j, mask):
            p0 = pl.multiple_of(j * tk, tk)
            mask_fn, parts = (None, whole) if mask is None else mask
            for part in parts:
                carry = block(carry, p0, mask_fn, part)
            return carry

        init = tuple((jnp.full((1, tq), -3e38, F32), jnp.zeros((V_ROWS, tq), F32)) for _ in range(hg))
        carry = lax.fori_loop(lo, hi, lambda j, c: tile(c, j, loop_mask), init)
        for d, mask in enumerate(tail_masks):
            carry = tile(carry, hi + d, mask)
        return [acc[0:HEAD_DIM] * (1.0 / acc[HEAD_DIM:HEAD_DIM + 1]) for _, acc in carry]

    def causal(dist):
        return dist >= 0

    def window(dist):
        return (dist >= 0) & (dist < WINDOW)

    half_k, half_q = tk // 2, tq // 2
    square = tq == tk and half_q % LANES == 0
    diag_parts = ((0, half_k, 0, tq), (half_k, tk, half_q, tq)) if square else whole
    back_parts = ((0, half_k, 0, half_q), (half_k, tk, 0, tq)) if square and tk == WINDOW else whole

    o_slc = run(qa_sel, ksa_ref, vst_ref, 0, t0 // tk, None, [(causal, diag_parts)] * max(tq // tk, 1))

    o_win = run(qa_plain, kwa_ref, vwt_ref, jnp.maximum(t0 - WINDOW + 1, 0) // tk,
                (t0 + tq - 1) // tk, (window, back_parts), [(window, diag_parts)])

    gate_t = gate_ref[0].T
    for h in range(hg):
        o_t = (gate_t[h:h + 1] * o_cmp[h] + gate_t[hg + h:hg + h + 1] * o_slc[h]
               + gate_t[2 * hg + h:2 * hg + h + 1] * o_win[h])
        zc = slice(h * HEAD_DIM, (h + 1) * HEAD_DIM)
        o_ref[0, :, zc] = (o_t.T * z_ref[0, :, zc].astype(F32)).astype(o_ref.dtype)


def _attention(q, kv, kvcmp, gates, zs, batch, seq, tq=512, tk=512):
    tq, tk = min(tq, seq), min(tk, seq)
    g = N_KV_GROUPS
    hd = HEAD_DIM
    gw = HEADS_PER_GROUP * hd
    n_pad = seq // CMP_STRIDE
    n_cmp = (seq - CMP_BLOCK) // CMP_STRIDE + 1
    assert seq // SLC_BLOCK <= POS_COL and (tq % tk == 0 or tk % tq == 0) and WINDOW % tk == 0
    ksa, vst, kwa, vwt = _kv_prep(kv, batch, seq)
    ka_spec = pl.BlockSpec((1, 1, seq, 2 * hd), lambda b, gg, i: (b, gg, 0, 0))
    vt_spec = pl.BlockSpec((1, 1, V_ROWS, seq), lambda b, gg, i: (b, gg, 0, 0))
    return pl.pallas_call(
        functools.partial(_attn_kernel, tq=tq, tk=tk, n_cmp=n_cmp, seq=seq),
        grid=(batch, g, seq // tq),
        in_specs=[pl.BlockSpec((1, tq, gw), lambda b, gg, i: (b, i, gg)),
                  pl.BlockSpec((1, 1, n_pad, hd), lambda b, gg, i: (b, gg, 0, 0)),
                  pl.BlockSpec((1, 1, n_pad, hd), lambda b, gg, i: (batch + b, gg, 0, 0)),
                  ka_spec, vt_spec, ka_spec, vt_spec,
                  pl.BlockSpec((1, tq, LANES), lambda b, gg, i: (b, i, gg)),
                  pl.BlockSpec((1, tq, gw), lambda b, gg, i: (b, i, gg))],
        out_specs=pl.BlockSpec((1, tq, gw), lambda b, gg, i: (b, i, gg)),
        out_shape=jax.ShapeDtypeStruct((batch, seq, N_HEADS * hd), BF16),
        scratch_shapes=[pltpu.VMEM((n_pad, 2 * hd), BF16), pltpu.VMEM((hd, n_pad), BF16)],
        compiler_params=_params("parallel", "parallel", "arbitrary"),
        name="nsa_attention",
    )(q, kvcmp, kvcmp, ksa, vst, kwa, vwt, gates, zs)


def _conv_kernel(cur_ref, halo_ref, w_ref, b_ref, g_ref, beta_ref, z_ref, o_ref, sh_ref, *, rc):
    i = pl.program_id(1)
    ts = cur_ref.shape[1]
    sub = sh_ref.shape[0]
    sh_ref[0, 0:CONV_HALO, :] = jnp.where(i > 0, halo_ref[0], 0.0)
    sh_ref[0, CONV_HALO:CONV_HALO + ts, :] = cur_ref[0]
    n_sh = CONV_HALO + ts - sub
    for s in range(1, sub):
        sh_ref[s, 0:n_sh, :] = sh_ref[0, s:s + n_sh, :]
    first = CONV_HALO - (CONV_WIDTH - 1)
    for c in range(ts // rc):
        acc = jnp.zeros((rc, cur_ref.shape[2]), F32)
        for k in range(CONV_WIDTH):
            s = (first + k) % sub
            base = c * rc + first + k - s
            acc = acc + sh_ref[s, base:base + rc, :] * w_ref[k:k + 1, :]
        y = acc + b_ref[...]
        mu = jnp.mean(y, axis=-1, keepdims=True)
        d = y - mu
        var = jnp.mean(d * d, axis=-1, keepdims=True)
        y = d * lax.rsqrt(var + LN_EPS) * g_ref[...] + beta_ref[...]
        y = _silu(y) * z_ref[0, c * rc:(c + 1) * rc, :].astype(F32)
        o_ref[0, c * rc:(c + 1) * rc, :] = y.astype(o_ref.dtype)


def _conv_module(y_glu, zs, z_block, conv_w, conv_b, ln_g, ln_b, batch, seq, ts=256, rc=32):
    w = y_glu.shape[-1]
    ts = min(ts, seq)
    hb = ts // CONV_HALO
    vec = lambda a: a.reshape(1, w)
    vspec = pl.BlockSpec((1, w), lambda b, i: (0, 0))
    return pl.pallas_call(
        functools.partial(_conv_kernel, rc=rc),
        grid=(batch, seq // ts),
        in_specs=[pl.BlockSpec((1, ts, w), lambda b, i: (b, i, 0)),
                  pl.BlockSpec((1, CONV_HALO, w), lambda b, i: (b, jnp.maximum(i * hb - 1, 0), 0)),
                  pl.BlockSpec((CONV_WIDTH, w), lambda b, i: (0, 0)),
                  vspec, vspec, vspec,
                  pl.BlockSpec((1, ts, w), lambda b, i: (b, i, z_block))],
        out_specs=pl.BlockSpec((1, ts, w), lambda b, i: (b, i, 0)),
        out_shape=jax.ShapeDtypeStruct((batch, seq, w), BF16),
        scratch_shapes=[pltpu.VMEM((SUBLANES, CONV_HALO + ts, w), F32)],
        compiler_params=_params("parallel", "parallel"),
        name="conv_module",
    )(y_glu, y_glu, conv_w, vec(conv_b), vec(ln_g), vec(ln_b), zs)


def _sgu_kernel(u_ref, v_ref, z_ref, g_ref, beta_ref, w_ref, bt_ref, o_ref):
    ts, width = v_ref.shape
    n_groups = w_ref.shape[0]
    gw = width // n_groups
    v = v_ref[...]
    mu = jnp.mean(v, axis=-1, keepdims=True)
    d = v - mu
    var = jnp.mean(d * d, axis=-1, keepdims=True)
    vn = (d * lax.rsqrt(var + LN_EPS) * g_ref[...] + beta_ref[...]).astype(BF16)
    tri = (lax.broadcasted_iota(jnp.int32, (SGU_CHUNK, SGU_CHUNK), 0)
           >= lax.broadcasted_iota(jnp.int32, (SGU_CHUNK, SGU_CHUNK), 1))
    for gi in range(n_groups):
        w_g = jnp.where(tri, w_ref[gi], 0.0).astype(BF16)
        cols = slice(gi * gw, (gi + 1) * gw)
        for c in range(ts // SGU_CHUNK):
            rws = slice(c * SGU_CHUNK, (c + 1) * SGU_CHUNK)
            s = _dot(w_g, vn[rws, cols]) + bt_ref[:, gi:gi + 1]
            out = u_ref[rws, cols].astype(F32) * s * z_ref[rws, cols].astype(F32)
            o_ref[rws, cols] = out.astype(o_ref.dtype)


def _sgu_module(u_src, u_block, v_src, v_block, zs, z_block, ln_g, ln_b, sgu_w, sgu_b, ts=512):
    t = v_src.shape[0]
    n_groups = sgu_w.shape[0]
    width = n_groups * sgu_w.shape[1]
    ts = min(ts, t)
    vspec = pl.BlockSpec((1, width), lambda i: (0, 0))
    return pl.pallas_call(
        _sgu_kernel,
        grid=(t // ts,),
        in_specs=[pl.BlockSpec((ts, width), lambda i: (i, u_block)),
                  pl.BlockSpec((ts, width), lambda i: (i, v_block)),
                  pl.BlockSpec((ts, width), lambda i: (i, z_block)),
                  vspec, vspec,
                  pl.BlockSpec(sgu_w.shape, lambda i: (0, 0, 0)),
                  pl.BlockSpec((SGU_CHUNK, n_groups), lambda i: (0, 0))],
        out_specs=pl.BlockSpec((ts, width), lambda i: (i, 0)),
        out_shape=jax.ShapeDtypeStruct((t, width), BF16),
        compiler_params=_params("parallel"),
        name="sgu_module",
    )(u_src, v_src, zs, ln_g.reshape(1, width), ln_b.reshape(1, width), sgu_w, sgu_b.T)


def _merge_kernel(h_ref, oa_ref, ob_ref, oc_ref, wm0_ref, wm1_ref, wm2_ref,
                  woa_ref, wob_ref, woc_ref, y_ref):
    h = h_ref[...]
    y = _sigmoid(_dot_nt(h, wm0_ref[...])) * _dot(oa_ref[...], woa_ref[...])
    y = y + _sigmoid(_dot_nt(h, wm1_ref[...])) * _dot(ob_ref[...], wob_ref[...])
    y = y + _sigmoid(_dot_nt(h, wm2_ref[...])) * _dot(oc_ref[...], woc_ref[...])
    y_ref[...] = y.astype(y_ref.dtype)


def _merge(h, oa, ob, oc, w_t, row, w_o_a, w_o_b, w_o_c, tm=512, tn=512):
    t, d = h.shape
    tm, tn = min(tm, t), min(tn, d)
    nb = d // tn

    def act(a):
        return pl.BlockSpec((tm, a.shape[1]), lambda j, i: (i, 0))

    def wm(branch):
        return _row_window(tn, d, lambda j, i: row + branch * d + j * tn)

    def wo(a):
        return pl.BlockSpec((a.shape[0], tn), lambda j, i: (0, j))

    return pl.pallas_call(
        _merge_kernel,
        grid=(nb, t // tm),
        in_specs=[act(h), act(oa), act(ob), act(oc), wm(0), wm(1), wm(2),
                  wo(w_o_a), wo(w_o_b), wo(w_o_c)],
        out_specs=pl.BlockSpec((tm, tn), lambda j, i: (i, j)),
        out_shape=jax.ShapeDtypeStruct((t, d), BF16),
        compiler_params=_params("parallel", "parallel"),
        name="merge",
    )(h, oa, ob, oc, w_t, w_t, w_t, w_o_a, w_o_b, w_o_c)


def _out_kernel(y_ref, w_ref, x_ref, o_ref):
    o_ref[...] = x_ref[...] + _dot(y_ref[...], w_ref[...])


def _out_proj(y, w, x, tm=1024, tn=512):
    t, d = y.shape
    n = w.shape[1]
    tm, tn = min(tm, t), min(tn, n)
    return pl.pallas_call(
        _out_kernel,
        grid=(t // tm, n // tn),
        in_specs=[pl.BlockSpec((tm, d), lambda i, j: (i, 0)),
                  pl.BlockSpec((d, tn), lambda i, j: (0, j)),
                  pl.BlockSpec((tm, tn), lambda i, j: (i, j))],
        out_specs=pl.BlockSpec((tm, tn), lambda i, j: (i, j)),
        out_shape=jax.ShapeDtypeStruct((t, n), F32),
        compiler_params=_params("parallel", "parallel"),
        name="out_proj",
    )(y, w, x)


def _layer(x2d, batch, seq, layer, norm_g, w_in, cmp_pos, cmp_w1, cmp_w2, w_o_a, conv_w, conv_b,
           conv_ln_g, conv_ln_b, w_o_b, sgu_ln_g, sgu_ln_b, sgu_w, sgu_b, w_o_c, w_out):
    d = x2d.shape[1]
    a_w = w_o_a.shape[0]
    b_w = w_o_b.shape[0]
    c_w = w_o_c.shape[0]
    kv_w = N_KV_GROUPS * HEAD_DIM
    n_gate = N_NSA_BRANCHES * N_HEADS
    hg = HEADS_PER_GROUP

    o_q = 0
    o_kv = o_q + a_w
    o_gate = o_kv + 6 * kv_w
    o_za = o_gate + n_gate
    o_b = o_za + a_w
    o_zb = o_b + 2 * b_w
    o_c = o_zb + b_w
    o_zc = o_c + 2 * c_w
    o_m = o_zc + c_w

    w_t = jnp.swapaxes(w_in, 1, 2)[layer]
    wp = w_t.astype(BF16)
    wg = w_t[o_gate:o_gate + n_gate].reshape(N_NSA_BRANCHES, N_KV_GROUPS, hg, d)
    wg = wg.transpose(1, 0, 2, 3).reshape(N_KV_GROUPS, N_NSA_BRANCHES * hg, d)
    wg = jnp.pad(wg, ((0, 0), (0, LANES - N_NSA_BRANCHES * hg), (0, 0)))
    wg = wg.reshape(N_KV_GROUPS * LANES, d).astype(BF16)

    tn = 1024
    h = _rmsnorm(x2d, norm_g, BF16)
    p_q = _proj(h, wp, _tile_starts([(o_q, a_w)], tn), "scale", BF16, 1024, tn, scale=ATTN_SCALE * LOG2E)
    p_bf = _proj(h, wp, _tile_starts([(o_kv + 2 * kv_w, 4 * kv_w), (o_c, c_w)], tn),
                 "none", BF16, 1024, tn)
    p_f32 = _proj(h, wp, _tile_starts([(o_kv, 2 * kv_w), (o_c + c_w, c_w)], tn),
                  "none", F32, 1024, tn)
    p_silu = _proj(h, wp, _tile_starts([(o_za, a_w), (o_zb, b_w), (o_zc, c_w)], tn),
                   "silu", BF16, 1024, tn)
    p_gate = _proj(h, wg, (0,), "sigmoid", F32, 1024, N_KV_GROUPS * LANES)
    y_glu = _glu_proj(h, wp, o_b, o_b + b_w, b_w, 1024, 512)

    def b3(a):
        return a.reshape(batch, seq, a.shape[-1])

    kvcmp = _compress(b3(p_f32), cmp_pos, cmp_w1, cmp_w2, batch, seq)
    o_a = _attention(b3(p_q), b3(p_bf), kvcmp, b3(p_gate), b3(p_silu), batch, seq)
    o_b_ = _conv_module(b3(y_glu), b3(p_silu), a_w // b_w, conv_w, conv_b, conv_ln_g, conv_ln_b,
                        batch, seq)
    o_c_ = _sgu_module(p_bf, 4 * kv_w // c_w, p_f32, 2 * kv_w // c_w,
                       p_silu, (a_w + b_w) // c_w, sgu_ln_g, sgu_ln_b, sgu_w, sgu_b)
    y = _merge(h, o_a.reshape(-1, a_w), o_b_.reshape(-1, b_w), o_c_, wp, o_m,
               w_o_a.astype(BF16), w_o_b.astype(BF16), w_o_c.astype(BF16))
    return _out_proj(y, w_out.astype(BF16), x2d)


def kernel(x, norm_g, w_in, cmp_pos, cmp_w1, cmp_w2, w_o_a, conv_w, conv_b, conv_ln_g, conv_ln_b,
           w_o_b, sgu_ln_g, sgu_ln_b, sgu_w, sgu_b, w_o_c, w_out, final_g):
    batch, seq, d = x.shape
    x2d = x.reshape(batch * seq, d)
    for l in range(norm_g.shape[0]):
        x2d = _layer(x2d, batch, seq, l, norm_g[l], w_in, cmp_pos[l], cmp_w1[l], cmp_w2[l],
                     w_o_a[l], conv_w[l], conv_b[l], conv_ln_g[l], conv_ln_b[l], w_o_b[l],
                     sgu_ln_g[l], sgu_ln_b[l], sgu_w[l], sgu_b[l], w_o_c[l], w_out[l])
    return _rmsnorm(x2d, final_g, x.dtype).reshape(batch, seq, d)
```

```python
import functools

import jax
import jax.numpy as jnp
from jax import lax
from jax.experimental import pallas as pl
from jax.experimental.pallas import tpu as pltpu

F32 = jnp.float32
BF16 = jnp.bfloat16

N_HEADS = 16
HEAD_DIM = 128
N_KV_GROUPS = 4
HEADS_PER_GROUP = N_HEADS // N_KV_GROUPS
N_NSA_BRANCHES = 3
CMP_BLOCK = 32
CMP_STRIDE = 16
SLC_BLOCK = 64
N_SELECT = 16
WINDOW = 512
ATTN_SCALE = HEAD_DIM ** -0.5
FORCE_SCORE = 1e9
NEG_INF = -1e30
CONV_WIDTH = 31
SGU_CHUNK = 128
RMS_EPS = 1e-6
LN_EPS = 1e-5

LANES = 128
SUBLANES = 8
CONV_HALO = 32
VMEM_LIMIT = 56 * 1024 * 1024


def _params(*semantics):
    return pltpu.CompilerParams(dimension_semantics=semantics, vmem_limit_bytes=VMEM_LIMIT)


def _dot(a, b):
    return jnp.dot(a, b, preferred_element_type=F32)


def _dot_nt(a, b):
    return lax.dot_general(a, b, (((1,), (1,)), ((), ())), preferred_element_type=F32)


def _sigmoid(x):
    return 1.0 / (1.0 + jnp.exp(-x))


def _silu(x):
    return x * _sigmoid(x)


def _rmsnorm_kernel(x_ref, g_ref, o_ref):
    x = x_ref[...]
    ms = jnp.mean(x * x, axis=-1, keepdims=True)
    o_ref[...] = (x * lax.rsqrt(ms + RMS_EPS) * g_ref[...]).astype(o_ref.dtype)


def _rmsnorm(x2d, g, out_dtype, tr=512):
    t, d = x2d.shape
    return pl.pallas_call(
        _rmsnorm_kernel,
        grid=(t // tr,),
        in_specs=[pl.BlockSpec((tr, d), lambda i: (i, 0)),
                  pl.BlockSpec((1, d), lambda i: (0, 0))],
        out_specs=pl.BlockSpec((tr, d), lambda i: (i, 0)),
        out_shape=jax.ShapeDtypeStruct((t, d), out_dtype),
        compiler_params=_params("parallel"),
        name="rmsnorm",
    )(x2d, g.reshape(1, d))


def _proj_kernel(x_ref, w_ref, o_ref, *, act, scale):
    acc = _dot_nt(x_ref[...], w_ref[...])
    if act == "silu":
        acc = _silu(acc)
    elif act == "sigmoid":
        acc = _sigmoid(acc)
    elif act == "scale":
        acc = acc * scale
    o_ref[...] = acc.astype(o_ref.dtype)


def _pick(j, values):
    r = values[0]
    for idx, v in enumerate(values[1:], 1):
        r = jnp.where(j == idx, v, r)
    return r


ROW_ALIGN = 16


def _row_window(rows, k, start_fn):
    return pl.BlockSpec((pl.Element(rows), pl.Element(k)),
                        lambda *ids: (pl.multiple_of(start_fn(*ids), ROW_ALIGN), 0))


def _tile_starts(slabs, tn):
    assert all(width % tn == 0 for _, width in slabs)
    return tuple(start + r for start, width in slabs for r in range(0, width, tn))


def _proj(x, w_t, starts, act, out_dtype, tm, tn, scale=None):
    t, k = x.shape
    tm = min(tm, t)
    return pl.pallas_call(
        functools.partial(_proj_kernel, act=act, scale=scale),
        grid=(len(starts), t // tm),
        in_specs=[pl.BlockSpec((tm, k), lambda j, i: (i, 0)),
                  _row_window(tn, k, lambda j, i: _pick(j, starts))],
        out_specs=pl.BlockSpec((tm, tn), lambda j, i: (i, j)),
        out_shape=jax.ShapeDtypeStruct((t, len(starts) * tn), out_dtype),
        compiler_params=_params("parallel", "parallel"),
        name="proj_" + act,
    )(x, w_t)


def _glu_kernel(x_ref, wu_ref, wg_ref, o_ref):
    x = x_ref[...]
    o_ref[...] = (_dot_nt(x, wu_ref[...]) * _sigmoid(_dot_nt(x, wg_ref[...]))).astype(o_ref.dtype)


def _glu_proj(x, w_t, row_u, row_g, n, tm, tn):
    t, k = x.shape
    tm, tn = min(tm, t), min(tn, n)
    assert n % tn == 0
    return pl.pallas_call(
        _glu_kernel,
        grid=(n // tn, t // tm),
        in_specs=[pl.BlockSpec((tm, k), lambda j, i: (i, 0)),
                  _row_window(tn, k, lambda j, i: row_u + j * tn),
                  _row_window(tn, k, lambda j, i: row_g + j * tn)],
        out_specs=pl.BlockSpec((tm, tn), lambda j, i: (i, j)),
        out_shape=jax.ShapeDtypeStruct((t, n), F32),
        compiler_params=_params("parallel", "parallel"),
        name="proj_glu",
    )(x, w_t, w_t)


def _compress_kernel(kv_ref, pos_ref, w1_ref, w2_ref, o_ref, *, n_cmp):
    n_pad = o_ref.shape[2]
    half = CMP_BLOCK // 2
    lo = jnp.zeros((n_pad, HEAD_DIM), F32)
    hi = jnp.zeros((n_pad, HEAD_DIM), F32)
    for l in range(half):
        rows = kv_ref[0, pl.ds(l, n_pad, stride=CMP_STRIDE), :]
        a = (rows + pos_ref[0, l:l + 1, :]).astype(BF16)
        b = (rows + pos_ref[0, half + l:half + l + 1, :]).astype(BF16)
        lo = lo + _dot(a, w1_ref[0, l * HEAD_DIM:(l + 1) * HEAD_DIM, :])
        hi = hi + _dot(b, w1_ref[0, (half + l) * HEAD_DIM:(half + l + 1) * HEAD_DIM, :])
    hid = lo + pltpu.roll(hi, n_pad - 1, 0)
    out = _dot(_silu(hid).astype(BF16), w2_ref[0])
    valid = lax.broadcasted_iota(jnp.int32, (n_pad, 1), 0) < n_cmp
    o_ref[0, 0] = jnp.where(valid, out, 0.0).astype(o_ref.dtype)


def _compress(kvc, cmp_pos, cmp_w1, cmp_w2, batch, seq):
    n_pad = seq // CMP_STRIDE
    n_cmp = (seq - CMP_BLOCK) // CMP_STRIDE + 1
    g = N_KV_GROUPS
    return pl.pallas_call(
        functools.partial(_compress_kernel, n_cmp=n_cmp),
        grid=(2, batch, g),
        in_specs=[pl.BlockSpec((1, seq, HEAD_DIM), lambda i, b, gg: (b, 0, i * g + gg)),
                  pl.BlockSpec((1, CMP_BLOCK, HEAD_DIM), lambda i, b, gg: (i, 0, 0)),
                  pl.BlockSpec((1, CMP_BLOCK * HEAD_DIM, HEAD_DIM), lambda i, b, gg: (i, 0, 0)),
                  pl.BlockSpec((1, HEAD_DIM, HEAD_DIM), lambda i, b, gg: (i, 0, 0))],
        out_specs=pl.BlockSpec((1, 1, n_pad, HEAD_DIM), lambda i, b, gg: (i * batch + b, gg, 0, 0)),
        out_shape=jax.ShapeDtypeStruct((2 * batch, g, n_pad, HEAD_DIM), BF16),
        compiler_params=_params("parallel", "parallel", "parallel"),
        name="compress",
    )(kvc, cmp_pos, cmp_w1.astype(BF16), cmp_w2.astype(BF16))


LOG2E = 1.4426950408889634
POS_COL = SLC_BLOCK
N_PIECES = 3
V_ROWS = HEAD_DIM + ROW_ALIGN


def _key_extra(pos):
    n = pos.shape[0]
    col = lax.broadcasted_iota(jnp.int32, (n, HEAD_DIM), 1)
    blk = pos >> (SLC_BLOCK.bit_length() - 1)
    hi = (blk * SLC_BLOCK).astype(F32)
    lo = (pos & (SLC_BLOCK - 1)).astype(F32)
    onehot = jnp.where(col == blk, 1.0, 0.0)
    return jnp.where(col < POS_COL, onehot,
                     jnp.where(col < POS_COL + N_PIECES, hi,
                               jnp.where(col < POS_COL + 2 * N_PIECES, lo, 0.0)))


def _kvprep_kernel(ks_ref, vs_ref, kw_ref, vw_ref, ksa_ref, vst_ref, kwa_ref, vwt_ref):
    ts = ks_ref.shape[1]
    pos = pl.program_id(2) * ts + lax.broadcasted_iota(jnp.int32, (ts, 1), 0)
    extra = _key_extra(pos).astype(BF16)
    for src, dst in ((ks_ref, ksa_ref), (kw_ref, kwa_ref)):
        dst[0, 0, :, 0:HEAD_DIM] = src[0]
        dst[0, 0, :, HEAD_DIM:2 * HEAD_DIM] = extra
    ones_rows = jnp.where(lax.broadcasted_iota(jnp.int32, (V_ROWS - HEAD_DIM, ts), 0) == 0, 1.0, 0.0)
    for src, dst in ((vs_ref, vst_ref), (vw_ref, vwt_ref)):
        dst[0, 0, 0:HEAD_DIM, :] = src[0].astype(F32).T.astype(BF16)
        dst[0, 0, HEAD_DIM:V_ROWS, :] = ones_rows.astype(BF16)


def _kv_prep(kv, batch, seq, ts=2048):
    ts = min(ts, seq)
    g = N_KV_GROUPS
    hd = HEAD_DIM
    ka = jax.ShapeDtypeStruct((batch, g, seq, 2 * hd), BF16)
    vt = jax.ShapeDtypeStruct((batch, g, V_ROWS, seq), BF16)
    ka_spec = pl.BlockSpec((1, 1, ts, 2 * hd), lambda b, gg, i: (b, gg, i, 0))
    vt_spec = pl.BlockSpec((1, 1, V_ROWS, ts), lambda b, gg, i: (b, gg, 0, i))

    def part(p):
        return pl.BlockSpec((1, ts, hd), lambda b, gg, i: (b, i, p * g + gg))

    return pl.pallas_call(
        _kvprep_kernel,
        grid=(batch, g, seq // ts),
        in_specs=[part(0), part(1), part(2), part(3)],
        out_specs=[ka_spec, vt_spec, ka_spec, vt_spec],
        out_shape=[ka, vt, ka, vt],
        compiler_params=_params("parallel", "parallel", "parallel"),
        name="kv_prep",
    )(kv, kv, kv, kv)


def _attn_kernel(q_ref, kc_ref, vc_ref, ksa_ref, vst_ref, kwa_ref, vwt_ref, gate_ref, z_ref, o_ref,
                 kca_ref, vct_ref, *, tq, tk, n_cmp, seq):
    hg = HEADS_PER_GROUP
    g = pl.program_id(1)
    i = pl.program_id(2)
    t0 = i * tq
    n_sblk = seq // SLC_BLOCK
    n_sel = min(N_SELECT, n_sblk)
    n_pad = kc_ref.shape[2]
    nrow = lax.broadcasted_iota(jnp.int32, (n_pad, 1), 0)
    end_c = nrow * CMP_STRIDE + (CMP_BLOCK - 1)

    @pl.when(i == 0)
    def _():
        kca_ref[:, 0:HEAD_DIM] = kc_ref[0, 0]
        kca_ref[:, HEAD_DIM:2 * HEAD_DIM] = _key_extra(end_c).astype(BF16)
        vct_ref[...] = vc_ref[0, 0].astype(F32).T.astype(BF16)

    t_q = t0 + lax.broadcasted_iota(jnp.int32, (1, tq), 1)
    n_tail = HEAD_DIM - POS_COL
    kpos = lax.broadcasted_iota(jnp.int32, (n_tail, 1), 0)

    def slope_rows(h):
        slope2 = jnp.exp2(jnp.full((1, tq), hg * g + (h + 1), jnp.int32).astype(F32) * -0.5) * LOG2E
        piece0 = slope2.astype(BF16).astype(F32)
        rest = slope2 - piece0
        piece1 = rest.astype(BF16).astype(F32)
        piece2 = rest - piece1
        return jnp.where((kpos == 0) | (kpos == N_PIECES), piece0,
                         jnp.where((kpos == 1) | (kpos == N_PIECES + 1), piece1,
                                   jnp.where((kpos == 2) | (kpos == N_PIECES + 2), piece2, 0.0))).astype(BF16)

    q_t = [q_ref[0, :, h * HEAD_DIM:(h + 1) * HEAD_DIM].astype(F32).T.astype(BF16) for h in range(hg)]
    slopes = [slope_rows(h) for h in range(hg)]
    no_bias = jnp.zeros((POS_COL, tq), BF16)
    qa_plain = [jnp.concatenate([q_t[h], no_bias, slopes[h]], axis=0) for h in range(hg)]

    mask_c = (end_c <= t_q) & (nrow < n_cmp)
    kca = kca_ref[...]
    vct = vct_ref[...]
    o_cmp = []
    psum = jnp.zeros((n_pad, tq), F32)
    for h in range(hg):
        s = jnp.where(mask_c, _dot(kca, qa_plain[h]), NEG_INF)
        m = jnp.max(s, axis=0, keepdims=True)
        p = jnp.exp2(s - m)
        l = jnp.sum(p, axis=0, keepdims=True)
        inv = jnp.where(m > 0.5 * NEG_INF, 1.0 / l, 0.0)
        o_cmp.append(_dot(vct, p.astype(BF16)) * inv)
        psum = psum + p * inv

    jrow = lax.broadcasted_iota(jnp.int32, (LANES, n_pad), 0)
    nn = lax.broadcasted_iota(jnp.int32, (LANES, n_pad), 1)
    ovl_t = ((nn * CMP_STRIDE < jrow * SLC_BLOCK + SLC_BLOCK)
             & (nn * CMP_STRIDE + CMP_BLOCK > jrow * SLC_BLOCK)
             & (nn < n_cmp) & (jrow < n_sblk)).astype(F32)
    imp_t = jnp.dot(ovl_t, psum, precision=lax.Precision.HIGHEST, preferred_element_type=F32)
    imp = imp_t[:n_sblk]
    blk = lax.broadcasted_iota(jnp.int32, (n_sblk, tq), 0)
    cur = (t0 + lax.broadcasted_iota(jnp.int32, (n_sblk, tq), 1)) >> (SLC_BLOCK.bit_length() - 1)
    forced = (blk == 0) | (blk == cur) | (blk == cur - 1)
    val = jnp.where(forced, FORCE_SCORE, jnp.where(blk <= cur, imp, NEG_INF))
    sub = 8
    vals = [val[k * sub:(k + 1) * sub] for k in range(n_sblk // sub)]
    ranks = [jnp.zeros((sub, tq), jnp.int32) for _ in vals]
    sub_row = lax.broadcasted_iota(jnp.int32, (sub, tq), 0)
    for jp in range(n_sblk):
        v_jp = jnp.broadcast_to(val[jp:jp + 1, :], (sub, tq))
        for k, v_k in enumerate(vals):
            if (k + 1) * sub <= jp:
                ahead = jnp.where(v_jp > v_k, 1, 0)
            elif k * sub > jp:
                ahead = jnp.where(v_jp >= v_k, 1, 0)
            else:
                ahead = jnp.where(sub_row + k * sub > jp,
                                  jnp.where(v_jp >= v_k, 1, 0), jnp.where(v_jp > v_k, 1, 0))
            ranks[k] = ranks[k] + ahead
    rank = jnp.concatenate(ranks, axis=0)
    sel_bias = jnp.where(rank < n_sel, 0.0, NEG_INF).astype(BF16)
    if n_sblk < POS_COL:
        sel_bias = jnp.concatenate([sel_bias, jnp.zeros((POS_COL - n_sblk, tq), BF16)], axis=0)
    qa_sel = [jnp.concatenate([q_t[h], sel_bias, slopes[h]], axis=0) for h in range(hg)]

    rel = (lax.broadcasted_iota(jnp.int32, (1, tq), 1)
           - lax.broadcasted_iota(jnp.int32, (tk, 1), 0))

    whole = ((0, tk, 0, tq),)

    def run(qa, ka_ref, vt_ref, lo, hi, loop_mask, tail_masks):
        def block(carry, p0, mask_fn, part):
            k_lo, k_hi, q_lo, q_hi = part
            k_a = ka_ref[0, 0, pl.ds(p0 + k_lo, k_hi - k_lo), :]
            v_t = vt_ref[0, 0, :, pl.ds(p0 + k_lo, k_hi - k_lo)]
            mask = None if mask_fn is None else mask_fn(rel[k_lo:k_hi, q_lo:q_hi] + (t0 - p0))
            sc = [_dot(k_a, qa[h][:, q_lo:q_hi]) for h in range(hg)]
            probs, stats = [], []
            for h in range(hg):
                m_prev = carry[h][0][:, q_lo:q_hi]
                s = sc[h] if mask is None else jnp.where(mask, sc[h], NEG_INF)
                m_new = jnp.maximum(m_prev, jnp.max(s, axis=0, keepdims=True))
                probs.append(jnp.exp2((s - m_new).astype(BF16)))
                stats.append((m_new, jnp.exp2(m_prev - m_new)))
            new = []
            for h, (m_new, alpha) in enumerate(stats):
                acc = alpha * carry[h][1][:, q_lo:q_hi] + _dot(v_t, probs[h])
                if (q_lo, q_hi) != (0, tq):
                    m_new, acc = (
                        jnp.concatenate(([old[:, :q_lo]] if q_lo else []) + [part_new]
                                        + ([old[:, q_hi:]] if q_hi < tq else []), axis=1)
                        for old, part_new in zip(carry[h], (m_new, acc)))
                new.append((m_new, acc))
            return tuple(new)

        def tile(carry, j, mask):
            p0 = pl.multiple_of(j * tk, tk)
            mask_fn, parts = (None, whole) if mask is None else mask
            for part in parts:
                carry = block(carry, p0, mask_fn, part)
            return carry

        init = tuple((jnp.full((1, tq), -3e38, F32), jnp.zeros((V_ROWS, tq), F32)) for _ in range(hg))
        carry = lax.fori_loop(lo, hi, lambda j, c: tile(c, j, loop_mask), init)
        for d, mask in enumerate(tail_masks):
            carry = tile(carry, hi + d, mask)
        return [acc[0:HEAD_DIM] * (1.0 / acc[HEAD_DIM:HEAD_DIM + 1]) for _, acc in carry]

    def causal(dist):
        return dist >= 0

    def window(dist):
        return (dist >= 0) & (dist < WINDOW)

    half_k, half_q = tk // 2, tq // 2
    square = tq == tk and half_q % LANES == 0
    diag_parts = ((0, half_k, 0, tq), (half_k, tk, half_q, tq)) if square else whole
    back_parts = ((0, half_k, 0, half_q), (half_k, tk, 0, tq)) if square and tk == WINDOW else whole

    o_slc = run(qa_sel, ksa_ref, vst_ref, 0, t0 // tk, None, [(causal, diag_parts)] * max(tq // tk, 1))

    o_win = run(qa_plain, kwa_ref, vwt_ref, jnp.maximum(t0 - WINDOW + 1, 0) // tk,
                (t0 + tq - 1) // tk, (window, back_parts), [(window, diag_parts)])

    gate_t = gate_ref[0].T
    for h in range(hg):
        o_t = (gate_t[h:h + 1] * o_cmp[h] + gate_t[hg + h:hg + h + 1] * o_slc[h]
               + gate_t[2 * hg + h:2 * hg + h + 1] * o_win[h])
        zc = slice(h * HEAD_DIM, (h + 1) * HEAD_DIM)
        o_ref[0, :, zc] = (o_t.T * z_ref[0, :, zc].astype(F32)).astype(o_ref.dtype)


def _attention(q, kv, kvcmp, gates, zs, batch, seq, tq=512, tk=512):
    tq, tk = min(tq, seq), min(tk, seq)
    g = N_KV_GROUPS
    hd = HEAD_DIM
    gw = HEADS_PER_GROUP * hd
    n_pad = seq // CMP_STRIDE
    n_cmp = (seq - CMP_BLOCK) // CMP_STRIDE + 1
    assert seq // SLC_BLOCK <= POS_COL and (tq % tk == 0 or tk % tq == 0) and WINDOW % tk == 0
    ksa, vst, kwa, vwt = _kv_prep(kv, batch, seq)
    ka_spec = pl.BlockSpec((1, 1, seq, 2 * hd), lambda b, gg, i: (b, gg, 0, 0))
    vt_spec = pl.BlockSpec((1, 1, V_ROWS, seq), lambda b, gg, i: (b, gg, 0, 0))
    return pl.pallas_call(
        functools.partial(_attn_kernel, tq=tq, tk=tk, n_cmp=n_cmp, seq=seq),
        grid=(batch, g, seq // tq),
        in_specs=[pl.BlockSpec((1, tq, gw), lambda b, gg, i: (b, i, gg)),
                  pl.BlockSpec((1, 1, n_pad, hd), lambda b, gg, i: (b, gg, 0, 0)),
                  pl.BlockSpec((1, 1, n_pad, hd), lambda b, gg, i: (batch + b, gg, 0, 0)),
                  ka_spec, vt_spec, ka_spec, vt_spec,
                  pl.BlockSpec((1, tq, LANES), lambda b, gg, i: (b, i, gg)),
                  pl.BlockSpec((1, tq, gw), lambda b, gg, i: (b, i, gg))],
        out_specs=pl.BlockSpec((1, tq, gw), lambda b, gg, i: (b, i, gg)),
        out_shape=jax.ShapeDtypeStruct((batch, seq, N_HEADS * hd), BF16),
        scratch_shapes=[pltpu.VMEM((n_pad, 2 * hd), BF16), pltpu.VMEM((hd, n_pad), BF16)],
        compiler_params=_params("parallel", "parallel", "arbitrary"),
        name="nsa_attention",
    )(q, kvcmp, kvcmp, ksa, vst, kwa, vwt, gates, zs)


def _conv_kernel(cur_ref, halo_ref, w_ref, b_ref, g_ref, beta_ref, z_ref, o_ref, sh_ref, *, rc):
    i = pl.program_id(1)
    ts = cur_ref.shape[1]
    sub = sh_ref.shape[0]
    sh_ref[0, 0:CONV_HALO, :] = jnp.where(i > 0, halo_ref[0], 0.0)
    sh_ref[0, CONV_HALO:CONV_HALO + ts, :] = cur_ref[0]
    n_sh = CONV_HALO + ts - sub
    for s in range(1, sub):
        sh_ref[s, 0:n_sh, :] = sh_ref[0, s:s + n_sh, :]
    first = CONV_HALO - (CONV_WIDTH - 1)
    for c in range(ts // rc):
        acc = jnp.zeros((rc, cur_ref.shape[2]), F32)
        for k in range(CONV_WIDTH):
            s = (first + k) % sub
            base = c * rc + first + k - s
            acc = acc + sh_ref[s, base:base + rc, :] * w_ref[k:k + 1, :]
        y = acc + b_ref[...]
        mu = jnp.mean(y, axis=-1, keepdims=True)
        d = y - mu
        var = jnp.mean(d * d, axis=-1, keepdims=True)
        y = d * lax.rsqrt(var + LN_EPS) * g_ref[...] + beta_ref[...]
        y = _silu(y) * z_ref[0, c * rc:(c + 1) * rc, :].astype(F32)
        o_ref[0, c * rc:(c + 1) * rc, :] = y.astype(o_ref.dtype)


def _conv_module(y_glu, zs, z_block, conv_w, conv_b, ln_g, ln_b, batch, seq, ts=256, rc=32):
    w = y_glu.shape[-1]
    ts = min(ts, seq)
    hb = ts // CONV_HALO
    vec = lambda a: a.reshape(1, w)
    vspec = pl.BlockSpec((1, w), lambda b, i: (0, 0))
    return pl.pallas_call(
        functools.partial(_conv_kernel, rc=rc),
        grid=(batch, seq // ts),
        in_specs=[pl.BlockSpec((1, ts, w), lambda b, i: (b, i, 0)),
                  pl.BlockSpec((1, CONV_HALO, w), lambda b, i: (b, jnp.maximum(i * hb - 1, 0), 0)),
                  pl.BlockSpec((CONV_WIDTH, w), lambda b, i: (0, 0)),
                  vspec, vspec, vspec,
                  pl.BlockSpec((1, ts, w), lambda b, i: (b, i, z_block))],
        out_specs=pl.BlockSpec((1, ts, w), lambda b, i: (b, i, 0)),
        out_shape=jax.ShapeDtypeStruct((batch, seq, w), BF16),
        scratch_shapes=[pltpu.VMEM((SUBLANES, CONV_HALO + ts, w), F32)],
        compiler_params=_params("parallel", "parallel"),
        name="conv_module",
    )(y_glu, y_glu, conv_w, vec(conv_b), vec(ln_g), vec(ln_b), zs)


def _sgu_kernel(u_ref, v_ref, z_ref, g_ref, beta_ref, w_ref, bt_ref, o_ref):
    ts, width = v_ref.shape
    n_groups = w_ref.shape[0]
    gw = width // n_groups
    v = v_ref[...]
    mu = jnp.mean(v, axis=-1, keepdims=True)
    d = v - mu
    var = jnp.mean(d * d, axis=-1, keepdims=True)
    vn = (d * lax.rsqrt(var + LN_EPS) * g_ref[...] + beta_ref[...]).astype(BF16)
    tri = (lax.broadcasted_iota(jnp.int32, (SGU_CHUNK, SGU_CHUNK), 0)
           >= lax.broadcasted_iota(jnp.int32, (SGU_CHUNK, SGU_CHUNK), 1))
    for gi in range(n_groups):
        w_g = jnp.where(tri, w_ref[gi], 0.0).astype(BF16)
        cols = slice(gi * gw, (gi + 1) * gw)
        for c in range(ts // SGU_CHUNK):
            rws = slice(c * SGU_CHUNK, (c + 1) * SGU_CHUNK)
            s = _dot(w_g, vn[rws, cols]) + bt_ref[:, gi:gi + 1]
            out = u_ref[rws, cols].astype(F32) * s * z_ref[rws, cols].astype(F32)
            o_ref[rws, cols] = out.astype(o_ref.dtype)


def _sgu_module(u_src, u_block, v_src, v_block, zs, z_block, ln_g, ln_b, sgu_w, sgu_b, ts=512):
    t = v_src.shape[0]
    n_groups = sgu_w.shape[0]
    width = n_groups * sgu_w.shape[1]
    ts = min(ts, t)
    vspec = pl.BlockSpec((1, width), lambda i: (0, 0))
    return pl.pallas_call(
        _sgu_kernel,
        grid=(t // ts,),
        in_specs=[pl.BlockSpec((ts, width), lambda i: (i, u_block)),
                  pl.BlockSpec((ts, width), lambda i: (i, v_block)),
                  pl.BlockSpec((ts, width), lambda i: (i, z_block)),
                  vspec, vspec,
                  pl.BlockSpec(sgu_w.shape, lambda i: (0, 0, 0)),
                  pl.BlockSpec((SGU_CHUNK, n_groups), lambda i: (0, 0))],
        out_specs=pl.BlockSpec((ts, width), lambda i: (i, 0)),
        out_shape=jax.ShapeDtypeStruct((t, width), BF16),
        compiler_params=_params("parallel"),
        name="sgu_module",
    )(u_src, v_src, zs, ln_g.reshape(1, width), ln_b.reshape(1, width), sgu_w, sgu_b.T)


def _merge_kernel(h_ref, oa_ref, ob_ref, oc_ref, wm0_ref, wm1_ref, wm2_ref,
                  woa_ref, wob_ref, woc_ref, y_ref):
    h = h_ref[...]
    y = _sigmoid(_dot_nt(h, wm0_ref[...])) * _dot(oa_ref[...], woa_ref[...])
    y = y + _sigmoid(_dot_nt(h, wm1_ref[...])) * _dot(ob_ref[...], wob_ref[...])
    y = y + _sigmoid(_dot_nt(h, wm2_ref[...])) * _dot(oc_ref[...], woc_ref[...])
    y_ref[...] = y.astype(y_ref.dtype)


def _merge(h, oa, ob, oc, w_t, row, w_o_a, w_o_b, w_o_c, tm=512, tn=512):
    t, d = h.shape
    tm, tn = min(tm, t), min(tn, d)
    nb = d // tn

    def act(a):
        return pl.BlockSpec((tm, a.shape[1]), lambda j, i: (i, 0))

    def wm(branch):
        return _row_window(tn, d, lambda j, i: row + branch * d + j * tn)

    def wo(a):
        return pl.BlockSpec((a.shape[0], tn), lambda j, i: (0, j))

    return pl.pallas_call(
        _merge_kernel,
        grid=(nb, t // tm),
        in_specs=[act(h), act(oa), act(ob), act(oc), wm(0), wm(1), wm(2),
                  wo(w_o_a), wo(w_o_b), wo(w_o_c)],
        out_specs=pl.BlockSpec((tm, tn), lambda j, i: (i, j)),
        out_shape=jax.ShapeDtypeStruct((t, d), BF16),
        compiler_params=_params("parallel", "parallel"),
        name="merge",
    )(h, oa, ob, oc, w_t, w_t, w_t, w_o_a, w_o_b, w_o_c)


def _out_kernel(y_ref, w_ref, x_ref, o_ref):
    o_ref[...] = x_ref[...] + _dot(y_ref[...], w_ref[...])


def _out_proj(y, w, x, tm=1024, tn=512):
    t, d = y.shape
    n = w.shape[1]
    tm, tn = min(tm, t), min(tn, n)
    return pl.pallas_call(
        _out_kernel,
        grid=(t // tm, n // tn),
        in_specs=[pl.BlockSpec((tm, d), lambda i, j: (i, 0)),
                  pl.BlockSpec((d, tn), lambda i, j: (0, j)),
                  pl.BlockSpec((tm, tn), lambda i, j: (i, j))],
        out_specs=pl.BlockSpec((tm, tn), lambda i, j: (i, j)),
        out_shape=jax.ShapeDtypeStruct((t, n), F32),
        compiler_params=_params("parallel", "parallel"),
        name="out_proj",
    )(y, w, x)


def _layer(x2d, batch, seq, layer, norm_g, w_in, cmp_pos, cmp_w1, cmp_w2, w_o_a, conv_w, conv_b,
           conv_ln_g, conv_ln_b, w_o_b, sgu_ln_g, sgu_ln_b, sgu_w, sgu_b, w_o_c, w_out):
    d = x2d.shape[1]
    a_w = w_o_a.shape[0]
    b_w = w_o_b.shape[0]
    c_w = w_o_c.shape[0]
    kv_w = N_KV_GROUPS * HEAD_DIM
    n_gate = N_NSA_BRANCHES * N_HEADS
    hg = HEADS_PER_GROUP

    o_q = 0
    o_kv = o_q + a_w
    o_gate = o_kv + 6 * kv_w
    o_za = o_gate + n_gate
    o_b = o_za + a_w
    o_zb = o_b + 2 * b_w
    o_c = o_zb + b_w
    o_zc = o_c + 2 * c_w
    o_m = o_zc + c_w

    w_t = jnp.swapaxes(w_in, 1, 2)[layer]
    wp = w_t.astype(BF16)
    wg = w_t[o_gate:o_gate + n_gate].reshape(N_NSA_BRANCHES, N_KV_GROUPS, hg, d)
    wg = wg.transpose(1, 0, 2, 3).reshape(N_KV_GROUPS, N_NSA_BRANCHES * hg, d)
    wg = jnp.pad(wg, ((0, 0), (0, LANES - N_NSA_BRANCHES * hg), (0, 0)))
    wg = wg.reshape(N_KV_GROUPS * LANES, d).astype(BF16)

    tn = 1024
    h = _rmsnorm(x2d, norm_g, BF16)
    p_q = _proj(h, wp, _tile_starts([(o_q, a_w)], tn), "scale", BF16, 1024, tn, scale=ATTN_SCALE * LOG2E)
    p_bf = _proj(h, wp, _tile_starts([(o_kv + 2 * kv_w, 4 * kv_w), (o_c, c_w)], tn),
                 "none", BF16, 1024, tn)
    p_f32 = _proj(h, wp, _tile_starts([(o_kv, 2 * kv_w), (o_c + c_w, c_w)], tn),
                  "none", F32, 1024, tn)
    p_silu = _proj(h, wp, _tile_starts([(o_za, a_w), (o_zb, b_w), (o_zc, c_w)], tn),
                   "silu", BF16, 1024, tn)
    p_gate = _proj(h, wg, (0,), "sigmoid", F32, 1024, N_KV_GROUPS * LANES)
    y_glu = _glu_proj(h, wp, o_b, o_b + b_w, b_w, 1024, 512)

    def b3(a):
        return a.reshape(batch, seq, a.shape[-1])

    kvcmp = _compress(b3(p_f32), cmp_pos, cmp_w1, cmp_w2, batch, seq)
    o_a = _attention(b3(p_q), b3(p_bf), kvcmp, b3(p_gate), b3(p_silu), batch, seq)
    o_b_ = _conv_module(b3(y_glu), b3(p_silu), a_w // b_w, conv_w, conv_b, conv_ln_g, conv_ln_b,
                        batch, seq)
    o_c_ = _sgu_module(p_bf, 4 * kv_w // c_w, p_f32, 2 * kv_w // c_w,
                       p_silu, (a_w + b_w) // c_w, sgu_ln_g, sgu_ln_b, sgu_w, sgu_b)
    y = _merge(h, o_a.reshape(-1, a_w), o_b_.reshape(-1, b_w), o_c_, wp, o_m,
               w_o_a.astype(BF16), w_o_b.astype(BF16), w_o_c.astype(BF16))
    return _out_proj(y, w_out.astype(BF16), x2d)


def kernel(x, norm_g, w_in, cmp_pos, cmp_w1, cmp_w2, w_o_a, conv_w, conv_b, conv_ln_g, conv_ln_b,
           w_o_b, sgu_ln_g, sgu_ln_b, sgu_w, sgu_b, w_o_c, w_out, final_g):
    batch, seq, d = x.shape
    x2d = x.reshape(batch * seq, d)
    for l in range(norm_g.shape[0]):
        x2d = _layer(x2d, batch, seq, l, norm_g[l], w_in, cmp_pos[l], cmp_w1[l], cmp_w2[l],
                     w_o_a[l], conv_w[l], conv_b[l], conv_ln_g[l], conv_ln_b[l], w_o_b[l],
                     sgu_ln_g[l], sgu_ln_b[l], sgu_w[l], sgu_b[l], w_o_c[l], w_out[l])
    return _rmsnorm(x2d, final_g, x.dtype).reshape(batch, seq, d)
```

```python
import functools

import jax
import jax.numpy as jnp
from jax import lax
from jax.experimental import pallas as pl
from jax.experimental.pallas import tpu as pltpu

F32 = jnp.float32
BF16 = jnp.bfloat16

N_HEADS = 16
HEAD_DIM = 128
N_KV_GROUPS = 4
HEADS_PER_GROUP = N_HEADS // N_KV_GROUPS
N_NSA_BRANCHES = 3
CMP_BLOCK = 32
CMP_STRIDE = 16
SLC_BLOCK = 64
N_SELECT = 16
WINDOW = 512
ATTN_SCALE = HEAD_DIM ** -0.5
FORCE_SCORE = 1e9
NEG_INF = -1e30
CONV_WIDTH = 31
SGU_CHUNK = 128
RMS_EPS = 1e-6
LN_EPS = 1e-5

LANES = 128
SUBLANES = 8
CONV_HALO = 32
VMEM_LIMIT = 56 * 1024 * 1024

T_NORM = 512
T_PROJ = (1024, 1024)
T_GLU = (1024, 512)
T_MERGE = (512, 512)
T_OUT = (1024, 1024)
T_ATTN = (512, 512)
T_KV_PREP = 2048
T_CONV = (256, 32)
T_SGU = 512


def _params(*semantics):
    return pltpu.CompilerParams(dimension_semantics=semantics, vmem_limit_bytes=VMEM_LIMIT)


def _dot(a, b):
    return jnp.dot(a, b, preferred_element_type=F32)


def _dot_nt(a, b):
    return lax.dot_general(a, b, (((1,), (1,)), ((), ())), preferred_element_type=F32)


def _sigmoid(x):
    return 1.0 / (1.0 + jnp.exp(-x))


def _silu(x):
    return x * _sigmoid(x)


def _rmsnorm_kernel(x_ref, g_ref, o_ref):
    x = x_ref[...]
    ms = jnp.mean(x * x, axis=-1, keepdims=True)
    o_ref[...] = (x * lax.rsqrt(ms + RMS_EPS) * g_ref[...]).astype(o_ref.dtype)


def _rmsnorm(x2d, g, out_dtype, tr=T_NORM):
    t, d = x2d.shape
    return pl.pallas_call(
        _rmsnorm_kernel,
        grid=(t // tr,),
        in_specs=[pl.BlockSpec((tr, d), lambda i: (i, 0)),
                  pl.BlockSpec((1, d), lambda i: (0, 0))],
        out_specs=pl.BlockSpec((tr, d), lambda i: (i, 0)),
        out_shape=jax.ShapeDtypeStruct((t, d), out_dtype),
        compiler_params=_params("parallel"),
        name="rmsnorm",
    )(x2d, g.reshape(1, d))


def _proj_kernel(x_ref, w_ref, o_ref, *, act, scale):
    acc = _dot_nt(x_ref[...], w_ref[...])
    if act == "silu":
        acc = _silu(acc)
    elif act == "sigmoid":
        acc = _sigmoid(acc)
    elif act == "scale":
        acc = acc * scale
    o_ref[...] = acc.astype(o_ref.dtype)


def _pick(j, values):
    r = values[0]
    for idx, v in enumerate(values[1:], 1):
        r = jnp.where(j == idx, v, r)
    return r


ROW_ALIGN = 16


def _row_window(rows, k, start_fn):
    return pl.BlockSpec((pl.Element(rows), pl.Element(k)),
                        lambda *ids: (pl.multiple_of(start_fn(*ids), ROW_ALIGN), 0))


def _tile_starts(slabs, tn):
    assert all(width % tn == 0 for _, width in slabs)
    return tuple(start + r for start, width in slabs for r in range(0, width, tn))


def _proj(x, w_t, starts, act, out_dtype, tm, tn, scale=None):
    t, k = x.shape
    tm = min(tm, t)
    return pl.pallas_call(
        functools.partial(_proj_kernel, act=act, scale=scale),
        grid=(len(starts), t // tm),
        in_specs=[pl.BlockSpec((tm, k), lambda j, i: (i, 0)),
                  _row_window(tn, k, lambda j, i: _pick(j, starts))],
        out_specs=pl.BlockSpec((tm, tn), lambda j, i: (i, j)),
        out_shape=jax.ShapeDtypeStruct((t, len(starts) * tn), out_dtype),
        compiler_params=_params("parallel", "parallel"),
        name="proj_" + act,
    )(x, w_t)


def _glu_kernel(x_ref, wu_ref, wg_ref, o_ref):
    x = x_ref[...]
    o_ref[...] = (_dot_nt(x, wu_ref[...]) * _sigmoid(_dot_nt(x, wg_ref[...]))).astype(o_ref.dtype)


def _glu_proj(x, w_t, row_u, row_g, n, tm, tn):
    t, k = x.shape
    tm, tn = min(tm, t), min(tn, n)
    assert n % tn == 0
    return pl.pallas_call(
        _glu_kernel,
        grid=(n // tn, t // tm),
        in_specs=[pl.BlockSpec((tm, k), lambda j, i: (i, 0)),
                  _row_window(tn, k, lambda j, i: row_u + j * tn),
                  _row_window(tn, k, lambda j, i: row_g + j * tn)],
        out_specs=pl.BlockSpec((tm, tn), lambda j, i: (i, j)),
        out_shape=jax.ShapeDtypeStruct((t, n), F32),
        compiler_params=_params("parallel", "parallel"),
        name="proj_glu",
    )(x, w_t, w_t)


def _compress_kernel(kv_ref, pos_ref, w1_ref, w2_ref, o_ref, *, n_cmp):
    n_pad = o_ref.shape[2]
    half = CMP_BLOCK // 2
    lo = jnp.zeros((n_pad, HEAD_DIM), F32)
    hi = jnp.zeros((n_pad, HEAD_DIM), F32)
    for l in range(half):
        rows = kv_ref[0, pl.ds(l, n_pad, stride=CMP_STRIDE), :]
        a = (rows + pos_ref[0, l:l + 1, :]).astype(BF16)
        b = (rows + pos_ref[0, half + l:half + l + 1, :]).astype(BF16)
        lo = lo + _dot(a, w1_ref[0, l * HEAD_DIM:(l + 1) * HEAD_DIM, :])
        hi = hi + _dot(b, w1_ref[0, (half + l) * HEAD_DIM:(half + l + 1) * HEAD_DIM, :])
    hid = lo + pltpu.roll(hi, n_pad - 1, 0)
    out = _dot(_silu(hid).astype(BF16), w2_ref[0])
    valid = lax.broadcasted_iota(jnp.int32, (n_pad, 1), 0) < n_cmp
    o_ref[0, 0] = jnp.where(valid, out, 0.0).astype(o_ref.dtype)


def _compress(kvc, cmp_pos, cmp_w1, cmp_w2, batch, seq):
    n_pad = seq // CMP_STRIDE
    n_cmp = (seq - CMP_BLOCK) // CMP_STRIDE + 1
    g = N_KV_GROUPS
    return pl.pallas_call(
        functools.partial(_compress_kernel, n_cmp=n_cmp),
        grid=(2, batch, g),
        in_specs=[pl.BlockSpec((1, seq, HEAD_DIM), lambda i, b, gg: (b, 0, i * g + gg)),
                  pl.BlockSpec((1, CMP_BLOCK, HEAD_DIM), lambda i, b, gg: (i, 0, 0)),
                  pl.BlockSpec((1, CMP_BLOCK * HEAD_DIM, HEAD_DIM), lambda i, b, gg: (i, 0, 0)),
                  pl.BlockSpec((1, HEAD_DIM, HEAD_DIM), lambda i, b, gg: (i, 0, 0))],
        out_specs=pl.BlockSpec((1, 1, n_pad, HEAD_DIM), lambda i, b, gg: (i * batch + b, gg, 0, 0)),
        out_shape=jax.ShapeDtypeStruct((2 * batch, g, n_pad, HEAD_DIM), BF16),
        compiler_params=_params("parallel", "parallel", "parallel"),
        name="compress",
    )(kvc, cmp_pos, cmp_w1.astype(BF16), cmp_w2.astype(BF16))


LOG2E = 1.4426950408889634
POS_COL = SLC_BLOCK
N_PIECES = 3
V_ROWS = HEAD_DIM + ROW_ALIGN


def _key_extra(pos):
    n = pos.shape[0]
    col = lax.broadcasted_iota(jnp.int32, (n, HEAD_DIM), 1)
    blk = pos >> (SLC_BLOCK.bit_length() - 1)
    hi = (blk * SLC_BLOCK).astype(F32)
    lo = (pos & (SLC_BLOCK - 1)).astype(F32)
    onehot = jnp.where(col == blk, 1.0, 0.0)
    return jnp.where(col < POS_COL, onehot,
                     jnp.where(col < POS_COL + N_PIECES, hi,
                               jnp.where(col < POS_COL + 2 * N_PIECES, lo, 0.0)))


def _kvprep_kernel(ks_ref, vs_ref, kw_ref, vw_ref, ksa_ref, vst_ref, kwa_ref, vwt_ref):
    ts = ks_ref.shape[1]
    pos = pl.program_id(2) * ts + lax.broadcasted_iota(jnp.int32, (ts, 1), 0)
    extra = _key_extra(pos).astype(BF16)
    for src, dst in ((ks_ref, ksa_ref), (kw_ref, kwa_ref)):
        dst[0, 0, :, 0:HEAD_DIM] = src[0]
        dst[0, 0, :, HEAD_DIM:2 * HEAD_DIM] = extra
    ones_rows = jnp.where(lax.broadcasted_iota(jnp.int32, (V_ROWS - HEAD_DIM, ts), 0) == 0, 1.0, 0.0)
    for src, dst in ((vs_ref, vst_ref), (vw_ref, vwt_ref)):
        dst[0, 0, 0:HEAD_DIM, :] = src[0].astype(F32).T.astype(BF16)
        dst[0, 0, HEAD_DIM:V_ROWS, :] = ones_rows.astype(BF16)


def _kv_prep(kv, batch, seq, ts=T_KV_PREP):
    ts = min(ts, seq)
    g = N_KV_GROUPS
    hd = HEAD_DIM
    ka = jax.ShapeDtypeStruct((batch, g, seq, 2 * hd), BF16)
    vt = jax.ShapeDtypeStruct((batch, g, V_ROWS, seq), BF16)
    ka_spec = pl.BlockSpec((1, 1, ts, 2 * hd), lambda b, gg, i: (b, gg, i, 0))
    vt_spec = pl.BlockSpec((1, 1, V_ROWS, ts), lambda b, gg, i: (b, gg, 0, i))

    def part(p):
        return pl.BlockSpec((1, ts, hd), lambda b, gg, i: (b, i, p * g + gg))

    return pl.pallas_call(
        _kvprep_kernel,
        grid=(batch, g, seq // ts),
        in_specs=[part(0), part(1), part(2), part(3)],
        out_specs=[ka_spec, vt_spec, ka_spec, vt_spec],
        out_shape=[ka, vt, ka, vt],
        compiler_params=_params("parallel", "parallel", "parallel"),
        name="kv_prep",
    )(kv, kv, kv, kv)


def _attn_kernel(q_ref, kc_ref, vc_ref, ksa_ref, vst_ref, kwa_ref, vwt_ref, gate_ref, z_ref, o_ref,
                 kca_ref, vct_ref, *, tq, tk, n_cmp, seq):
    hg = HEADS_PER_GROUP
    g = pl.program_id(1)
    i = pl.program_id(2)
    t0 = i * tq
    n_sblk = seq // SLC_BLOCK
    n_sel = min(N_SELECT, n_sblk)
    n_pad = kc_ref.shape[2]
    nrow = lax.broadcasted_iota(jnp.int32, (n_pad, 1), 0)
    end_c = nrow * CMP_STRIDE + (CMP_BLOCK - 1)

    @pl.when(i == 0)
    def _():
        kca_ref[:, 0:HEAD_DIM] = kc_ref[0, 0]
        kca_ref[:, HEAD_DIM:2 * HEAD_DIM] = _key_extra(end_c).astype(BF16)
        vct_ref[...] = vc_ref[0, 0].astype(F32).T.astype(BF16)

    t_q = t0 + lax.broadcasted_iota(jnp.int32, (1, tq), 1)
    n_tail = HEAD_DIM - POS_COL
    kpos = lax.broadcasted_iota(jnp.int32, (n_tail, 1), 0)

    def slope_rows(h):
        slope2 = jnp.exp2(jnp.full((1, tq), hg * g + (h + 1), jnp.int32).astype(F32) * -0.5) * LOG2E
        piece0 = slope2.astype(BF16).astype(F32)
        rest = slope2 - piece0
        piece1 = rest.astype(BF16).astype(F32)
        piece2 = rest - piece1
        return jnp.where((kpos == 0) | (kpos == N_PIECES), piece0,
                         jnp.where((kpos == 1) | (kpos == N_PIECES + 1), piece1,
                                   jnp.where((kpos == 2) | (kpos == N_PIECES + 2), piece2, 0.0))).astype(BF16)

    q_t = [q_ref[0, :, h * HEAD_DIM:(h + 1) * HEAD_DIM].astype(F32).T.astype(BF16) for h in range(hg)]
    slopes = [slope_rows(h) for h in range(hg)]
    no_bias = jnp.zeros((POS_COL, tq), BF16)
    qa_plain = [jnp.concatenate([q_t[h], no_bias, slopes[h]], axis=0) for h in range(hg)]

    mask_c = (end_c <= t_q) & (nrow < n_cmp)
    kca = kca_ref[...]
    vct = vct_ref[...]
    o_cmp = []
    psum = jnp.zeros((n_pad, tq), F32)
    for h in range(hg):
        s = jnp.where(mask_c, _dot(kca, qa_plain[h]), NEG_INF)
        m = jnp.max(s, axis=0, keepdims=True)
        p = jnp.exp2(s - m)
        l = jnp.sum(p, axis=0, keepdims=True)
        inv = jnp.where(m > 0.5 * NEG_INF, 1.0 / l, 0.0)
        o_cmp.append(_dot(vct, p.astype(BF16)) * inv)
        psum = psum + p * inv

    jrow = lax.broadcasted_iota(jnp.int32, (LANES, n_pad), 0)
    nn = lax.broadcasted_iota(jnp.int32, (LANES, n_pad), 1)
    ovl_t = ((nn * CMP_STRIDE < jrow * SLC_BLOCK + SLC_BLOCK)
             & (nn * CMP_STRIDE + CMP_BLOCK > jrow * SLC_BLOCK)
             & (nn < n_cmp) & (jrow < n_sblk)).astype(F32)
    imp_t = jnp.dot(ovl_t, psum, precision=lax.Precision.HIGHEST, preferred_element_type=F32)
    imp = imp_t[:n_sblk]
    blk = lax.broadcasted_iota(jnp.int32, (n_sblk, tq), 0)
    cur = (t0 + lax.broadcasted_iota(jnp.int32, (n_sblk, tq), 1)) >> (SLC_BLOCK.bit_length() - 1)
    forced = (blk == 0) | (blk == cur) | (blk == cur - 1)
    val = jnp.where(forced, FORCE_SCORE, jnp.where(blk <= cur, imp, NEG_INF))
    sub = 8
    vals = [val[k * sub:(k + 1) * sub] for k in range(n_sblk // sub)]
    ranks = [jnp.zeros((sub, tq), jnp.int32) for _ in vals]
    sub_row = lax.broadcasted_iota(jnp.int32, (sub, tq), 0)
    for jp in range(n_sblk):
        v_jp = jnp.broadcast_to(val[jp:jp + 1, :], (sub, tq))
        for k, v_k in enumerate(vals):
            if (k + 1) * sub <= jp:
                ahead = jnp.where(v_jp > v_k, 1, 0)
            elif k * sub > jp:
                ahead = jnp.where(v_jp >= v_k, 1, 0)
            else:
                ahead = jnp.where(sub_row + k * sub > jp,
                                  jnp.where(v_jp >= v_k, 1, 0), jnp.where(v_jp > v_k, 1, 0))
            ranks[k] = ranks[k] + ahead
    rank = jnp.concatenate(ranks, axis=0)
    sel_bias = jnp.where(rank < n_sel, 0.0, NEG_INF).astype(BF16)
    if n_sblk < POS_COL:
        sel_bias = jnp.concatenate([sel_bias, jnp.zeros((POS_COL - n_sblk, tq), BF16)], axis=0)
    qa_sel = [jnp.concatenate([q_t[h], sel_bias, slopes[h]], axis=0) for h in range(hg)]

    rel = (lax.broadcasted_iota(jnp.int32, (1, tq), 1)
           - lax.broadcasted_iota(jnp.int32, (tk, 1), 0))

    whole = ((0, tk, 0, tq),)

    def run(qa, ka_ref, vt_ref, lo, hi, loop_mask, tail_masks):
        def block(carry, p0, mask_fn, part):
            k_lo, k_hi, q_lo, q_hi = part
            k_a = ka_ref[0, 0, pl.ds(p0 + k_lo, k_hi - k_lo), :]
            v_t = vt_ref[0, 0, :, pl.ds(p0 + k_lo, k_hi - k_lo)]
            mask = None if mask_fn is None else mask_fn(rel[k_lo:k_hi, q_lo:q_hi] + (t0 - p0))
            sc = [_dot(k_a, qa[h][:, q_lo:q_hi]) for h in range(hg)]
            probs, stats = [], []
            for h in range(hg):
                m_prev = carry[h][0][:, q_lo:q_hi]
                s = sc[h] if mask is None else jnp.where(mask, sc[h], NEG_INF)
                m_new = jnp.maximum(m_prev, jnp.max(s, axis=0, keepdims=True))
                probs.append(jnp.exp2((s - m_new).astype(BF16)))
                stats.append((m_new, jnp.exp2(m_prev - m_new)))
            new = []
            for h, (m_new, alpha) in enumerate(stats):
                acc = alpha * carry[h][1][:, q_lo:q_hi] + _dot(v_t, probs[h])
                if (q_lo, q_hi) != (0, tq):
                    m_new, acc = (
                        jnp.concatenate(([old[:, :q_lo]] if q_lo else []) + [part_new]
                                        + ([old[:, q_hi:]] if q_hi < tq else []), axis=1)
                        for old, part_new in zip(carry[h], (m_new, acc)))
                new.append((m_new, acc))
            return tuple(new)

        def tile(carry, j, mask):
            p0 = pl.multiple_of(j * tk, tk)
            mask_fn, parts = (None, whole) if mask is None else mask
            for part in parts:
                carry = block(carry, p0, mask_fn, part)
            return carry

        init = tuple((jnp.full((1, tq), -3e38, F32), jnp.zeros((V_ROWS, tq), F32)) for _ in range(hg))
        carry = lax.fori_loop(lo, hi, lambda j, c: tile(c, j, loop_mask), init)
        for d, mask in enumerate(tail_masks):
            carry = tile(carry, hi + d, mask)
        return [acc[0:HEAD_DIM] * (1.0 / acc[HEAD_DIM:HEAD_DIM + 1]) for _, acc in carry]

    def causal(dist):
        return dist >= 0

    def window(dist):
        return (dist >= 0) & (dist < WINDOW)

    half_k, half_q = tk // 2, tq // 2
    square = tq == tk and half_q % LANES == 0
    diag_parts = ((0, half_k, 0, tq), (half_k, tk, half_q, tq)) if square else whole
    back_parts = ((0, half_k, 0, half_q), (half_k, tk, 0, tq)) if square and tk == WINDOW else whole

    o_slc = run(qa_sel, ksa_ref, vst_ref, 0, t0 // tk, None, [(causal, diag_parts)] * max(tq // tk, 1))

    o_win = run(qa_plain, kwa_ref, vwt_ref, jnp.maximum(t0 - WINDOW + 1, 0) // tk,
                (t0 + tq - 1) // tk, (window, back_parts), [(window, diag_parts)])

    gate_t = gate_ref[0].T
    for h in range(hg):
        o_t = (gate_t[h:h + 1] * o_cmp[h] + gate_t[hg + h:hg + h + 1] * o_slc[h]
               + gate_t[2 * hg + h:2 * hg + h + 1] * o_win[h])
        zc = slice(h * HEAD_DIM, (h + 1) * HEAD_DIM)
        o_ref[0, :, zc] = (o_t.T * z_ref[0, :, zc].astype(F32)).astype(o_ref.dtype)


def _attention(q, kv, kvcmp, gates, zs, batch, seq, tq=T_ATTN[0], tk=T_ATTN[1]):
    tq, tk = min(tq, seq), min(tk, seq)
    g = N_KV_GROUPS
    hd = HEAD_DIM
    gw = HEADS_PER_GROUP * hd
    n_pad = seq // CMP_STRIDE
    n_cmp = (seq - CMP_BLOCK) // CMP_STRIDE + 1
    assert seq // SLC_BLOCK <= POS_COL and (tq % tk == 0 or tk % tq == 0) and WINDOW % tk == 0
    ksa, vst, kwa, vwt = _kv_prep(kv, batch, seq)
    ka_spec = pl.BlockSpec((1, 1, seq, 2 * hd), lambda b, gg, i: (b, gg, 0, 0))
    vt_spec = pl.BlockSpec((1, 1, V_ROWS, seq), lambda b, gg, i: (b, gg, 0, 0))
    return pl.pallas_call(
        functools.partial(_attn_kernel, tq=tq, tk=tk, n_cmp=n_cmp, seq=seq),
        grid=(batch, g, seq // tq),
        in_specs=[pl.BlockSpec((1, tq, gw), lambda b, gg, i: (b, i, gg)),
                  pl.BlockSpec((1, 1, n_pad, hd), lambda b, gg, i: (b, gg, 0, 0)),
                  pl.BlockSpec((1, 1, n_pad, hd), lambda b, gg, i: (batch + b, gg, 0, 0)),
                  ka_spec, vt_spec, ka_spec, vt_spec,
                  pl.BlockSpec((1, tq, LANES), lambda b, gg, i: (b, i, gg)),
                  pl.BlockSpec((1, tq, gw), lambda b, gg, i: (b, i, gg))],
        out_specs=pl.BlockSpec((1, tq, gw), lambda b, gg, i: (b, i, gg)),
        out_shape=jax.ShapeDtypeStruct((batch, seq, N_HEADS * hd), BF16),
        scratch_shapes=[pltpu.VMEM((n_pad, 2 * hd), BF16), pltpu.VMEM((hd, n_pad), BF16)],
        compiler_params=_params("parallel", "parallel", "arbitrary"),
        name="nsa_attention",
    )(q, kvcmp, kvcmp, ksa, vst, kwa, vwt, gates, zs)


def _conv_kernel(cur_ref, halo_ref, w_ref, b_ref, g_ref, beta_ref, z_ref, o_ref, sh_ref, *, rc):
    i = pl.program_id(1)
    ts = cur_ref.shape[1]
    sub = sh_ref.shape[0]
    sh_ref[0, 0:CONV_HALO, :] = jnp.where(i > 0, halo_ref[0], 0.0)
    sh_ref[0, CONV_HALO:CONV_HALO + ts, :] = cur_ref[0]
    n_sh = CONV_HALO + ts - sub
    for s in range(1, sub):
        sh_ref[s, 0:n_sh, :] = sh_ref[0, s:s + n_sh, :]
    first = CONV_HALO - (CONV_WIDTH - 1)
    for c in range(ts // rc):
        acc = jnp.zeros((rc, cur_ref.shape[2]), F32)
        for k in range(CONV_WIDTH):
            s = (first + k) % sub
            base = c * rc + first + k - s
            acc = acc + sh_ref[s, base:base + rc, :] * w_ref[k:k + 1, :]
        y = acc + b_ref[...]
        mu = jnp.mean(y, axis=-1, keepdims=True)
        d = y - mu
        var = jnp.mean(d * d, axis=-1, keepdims=True)
        y = d * lax.rsqrt(var + LN_EPS) * g_ref[...] + beta_ref[...]
        y = _silu(y) * z_ref[0, c * rc:(c + 1) * rc, :].astype(F32)
        o_ref[0, c * rc:(c + 1) * rc, :] = y.astype(o_ref.dtype)


def _conv_module(y_glu, zs, z_block, conv_w, conv_b, ln_g, ln_b, batch, seq, ts=T_CONV[0], rc=T_CONV[1]):
    w = y_glu.shape[-1]
    ts = min(ts, seq)
    hb = ts // CONV_HALO
    vec = lambda a: a.reshape(1, w)
    vspec = pl.BlockSpec((1, w), lambda b, i: (0, 0))
    return pl.pallas_call(
        functools.partial(_conv_kernel, rc=rc),
        grid=(batch, seq // ts),
        in_specs=[pl.BlockSpec((1, ts, w), lambda b, i: (b, i, 0)),
                  pl.BlockSpec((1, CONV_HALO, w), lambda b, i: (b, jnp.maximum(i * hb - 1, 0), 0)),
                  pl.BlockSpec((CONV_WIDTH, w), lambda b, i: (0, 0)),
                  vspec, vspec, vspec,
                  pl.BlockSpec((1, ts, w), lambda b, i: (b, i, z_block))],
        out_specs=pl.BlockSpec((1, ts, w), lambda b, i: (b, i, 0)),
        out_shape=jax.ShapeDtypeStruct((batch, seq, w), BF16),
        scratch_shapes=[pltpu.VMEM((SUBLANES, CONV_HALO + ts, w), F32)],
        compiler_params=_params("parallel", "parallel"),
        name="conv_module",
    )(y_glu, y_glu, conv_w, vec(conv_b), vec(ln_g), vec(ln_b), zs)


def _sgu_kernel(u_ref, v_ref, z_ref, g_ref, beta_ref, w_ref, bt_ref, o_ref):
    ts, width = v_ref.shape
    n_groups = w_ref.shape[0]
    gw = width // n_groups
    v = v_ref[...]
    mu = jnp.mean(v, axis=-1, keepdims=True)
    d = v - mu
    var = jnp.mean(d * d, axis=-1, keepdims=True)
    vn = (d * lax.rsqrt(var + LN_EPS) * g_ref[...] + beta_ref[...]).astype(BF16)
    tri = (lax.broadcasted_iota(jnp.int32, (SGU_CHUNK, SGU_CHUNK), 0)
           >= lax.broadcasted_iota(jnp.int32, (SGU_CHUNK, SGU_CHUNK), 1))
    for gi in range(n_groups):
        w_g = jnp.where(tri, w_ref[gi], 0.0).astype(BF16)
        cols = slice(gi * gw, (gi + 1) * gw)
        for c in range(ts // SGU_CHUNK):
            rws = slice(c * SGU_CHUNK, (c + 1) * SGU_CHUNK)
            s = _dot(w_g, vn[rws, cols]) + bt_ref[:, gi:gi + 1]
            out = u_ref[rws, cols].astype(F32) * s * z_ref[rws, cols].astype(F32)
            o_ref[rws, cols] = out.astype(o_ref.dtype)


def _sgu_module(u_src, u_block, v_src, v_block, zs, z_block, ln_g, ln_b, sgu_w, sgu_b, ts=T_SGU):
    t = v_src.shape[0]
    n_groups = sgu_w.shape[0]
    width = n_groups * sgu_w.shape[1]
    ts = min(ts, t)
    vspec = pl.BlockSpec((1, width), lambda i: (0, 0))
    return pl.pallas_call(
        _sgu_kernel,
        grid=(t // ts,),
        in_specs=[pl.BlockSpec((ts, width), lambda i: (i, u_block)),
                  pl.BlockSpec((ts, width), lambda i: (i, v_block)),
                  pl.BlockSpec((ts, width), lambda i: (i, z_block)),
                  vspec, vspec,
                  pl.BlockSpec(sgu_w.shape, lambda i: (0, 0, 0)),
                  pl.BlockSpec((SGU_CHUNK, n_groups), lambda i: (0, 0))],
        out_specs=pl.BlockSpec((ts, width), lambda i: (i, 0)),
        out_shape=jax.ShapeDtypeStruct((t, width), BF16),
        compiler_params=_params("parallel"),
        name="sgu_module",
    )(u_src, v_src, zs, ln_g.reshape(1, width), ln_b.reshape(1, width), sgu_w, sgu_b.T)


def _merge_kernel(h_ref, oa_ref, ob_ref, oc_ref, wm0_ref, wm1_ref, wm2_ref,
                  woa_ref, wob_ref, woc_ref, y_ref):
    h = h_ref[...]
    y = _sigmoid(_dot_nt(h, wm0_ref[...])) * _dot(oa_ref[...], woa_ref[...])
    y = y + _sigmoid(_dot_nt(h, wm1_ref[...])) * _dot(ob_ref[...], wob_ref[...])
    y = y + _sigmoid(_dot_nt(h, wm2_ref[...])) * _dot(oc_ref[...], woc_ref[...])
    y_ref[...] = y.astype(y_ref.dtype)


def _merge(h, oa, ob, oc, w_t, row, w_o_a, w_o_b, w_o_c, tm=T_MERGE[0], tn=T_MERGE[1]):
    t, d = h.shape
    tm, tn = min(tm, t), min(tn, d)
    nb = d // tn

    def act(a):
        return pl.BlockSpec((tm, a.shape[1]), lambda j, i: (i, 0))

    def wm(branch):
        return _row_window(tn, d, lambda j, i: row + branch * d + j * tn)

    def wo(a):
        return pl.BlockSpec((a.shape[0], tn), lambda j, i: (0, j))

    return pl.pallas_call(
        _merge_kernel,
        grid=(nb, t // tm),
        in_specs=[act(h), act(oa), act(ob), act(oc), wm(0), wm(1), wm(2),
                  wo(w_o_a), wo(w_o_b), wo(w_o_c)],
        out_specs=pl.BlockSpec((tm, tn), lambda j, i: (i, j)),
        out_shape=jax.ShapeDtypeStruct((t, d), BF16),
        compiler_params=_params("parallel", "parallel"),
        name="merge",
    )(h, oa, ob, oc, w_t, w_t, w_t, w_o_a, w_o_b, w_o_c)


def _out_kernel(y_ref, w_ref, x_ref, o_ref):
    o_ref[...] = x_ref[...] + _dot(y_ref[...], w_ref[...])


def _out_proj(y, w, x, tm=T_OUT[0], tn=T_OUT[1]):
    t, d = y.shape
    n = w.shape[1]
    tm, tn = min(tm, t), min(tn, n)
    return pl.pallas_call(
        _out_kernel,
        grid=(t // tm, n // tn),
        in_specs=[pl.BlockSpec((tm, d), lambda i, j: (i, 0)),
                  pl.BlockSpec((d, tn), lambda i, j: (0, j)),
                  pl.BlockSpec((tm, tn), lambda i, j: (i, j))],
        out_specs=pl.BlockSpec((tm, tn), lambda i, j: (i, j)),
        out_shape=jax.ShapeDtypeStruct((t, n), F32),
        compiler_params=_params("parallel", "parallel"),
        name="out_proj",
    )(y, w, x)


def _layer(x2d, batch, seq, layer, norm_g, w_in, cmp_pos, cmp_w1, cmp_w2, w_o_a, conv_w, conv_b,
           conv_ln_g, conv_ln_b, w_o_b, sgu_ln_g, sgu_ln_b, sgu_w, sgu_b, w_o_c, w_out):
    d = x2d.shape[1]
    a_w = w_o_a.shape[0]
    b_w = w_o_b.shape[0]
    c_w = w_o_c.shape[0]
    kv_w = N_KV_GROUPS * HEAD_DIM
    n_gate = N_NSA_BRANCHES * N_HEADS
    hg = HEADS_PER_GROUP

    o_q = 0
    o_kv = o_q + a_w
    o_gate = o_kv + 6 * kv_w
    o_za = o_gate + n_gate
    o_b = o_za + a_w
    o_zb = o_b + 2 * b_w
    o_c = o_zb + b_w
    o_zc = o_c + 2 * c_w
    o_m = o_zc + c_w

    w_t = jnp.swapaxes(w_in, 1, 2)[layer]
    wp = w_t.astype(BF16)
    wg = w_t[o_gate:o_gate + n_gate].reshape(N_NSA_BRANCHES, N_KV_GROUPS, hg, d)
    wg = wg.transpose(1, 0, 2, 3).reshape(N_KV_GROUPS, N_NSA_BRANCHES * hg, d)
    wg = jnp.pad(wg, ((0, 0), (0, LANES - N_NSA_BRANCHES * hg), (0, 0)))
    wg = wg.reshape(N_KV_GROUPS * LANES, d).astype(BF16)

    tm, tn = T_PROJ
    h = _rmsnorm(x2d, norm_g, BF16)
    p_q = _proj(h, wp, _tile_starts([(o_q, a_w)], tn), "scale", BF16, tm, tn, scale=ATTN_SCALE * LOG2E)
    p_bf = _proj(h, wp, _tile_starts([(o_kv + 2 * kv_w, 4 * kv_w), (o_c, c_w)], tn),
                 "none", BF16, tm, tn)
    p_f32 = _proj(h, wp, _tile_starts([(o_kv, 2 * kv_w), (o_c + c_w, c_w)], tn),
                  "none", F32, tm, tn)
    p_silu = _proj(h, wp, _tile_starts([(o_za, a_w), (o_zb, b_w), (o_zc, c_w)], tn),
                   "silu", BF16, tm, tn)
    p_gate = _proj(h, wg, (0,), "sigmoid", F32, tm, N_KV_GROUPS * LANES)
    y_glu = _glu_proj(h, wp, o_b, o_b + b_w, b_w, *T_GLU)

    def b3(a):
        return a.reshape(batch, seq, a.shape[-1])

    kvcmp = _compress(b3(p_f32), cmp_pos, cmp_w1, cmp_w2, batch, seq)
    o_a = _attention(b3(p_q), b3(p_bf), kvcmp, b3(p_gate), b3(p_silu), batch, seq)
    o_b_ = _conv_module(b3(y_glu), b3(p_silu), a_w // b_w, conv_w, conv_b, conv_ln_g, conv_ln_b,
                        batch, seq)
    o_c_ = _sgu_module(p_bf, 4 * kv_w // c_w, p_f32, 2 * kv_w // c_w,
                       p_silu, (a_w + b_w) // c_w, sgu_ln_g, sgu_ln_b, sgu_w, sgu_b)
    y = _merge(h, o_a.reshape(-1, a_w), o_b_.reshape(-1, b_w), o_c_, wp, o_m,
               w_o_a.astype(BF16), w_o_b.astype(BF16), w_o_c.astype(BF16))
    return _out_proj(y, w_out.astype(BF16), x2d)


def kernel(x, norm_g, w_in, cmp_pos, cmp_w1, cmp_w2, w_o_a, conv_w, conv_b, conv_ln_g, conv_ln_b,
           w_o_b, sgu_ln_g, sgu_ln_b, sgu_w, sgu_b, w_o_c, w_out, final_g):
    batch, seq, d = x.shape
    x2d = x.reshape(batch * seq, d)
    for l in range(norm_g.shape[0]):
        x2d = _layer(x2d, batch, seq, l, norm_g[l], w_in, cmp_pos[l], cmp_w1[l], cmp_w2[l],
                     w_o_a[l], conv_w[l], conv_b[l], conv_ln_g[l], conv_ln_b[l], w_o_b[l],
                     sgu_ln_g[l], sgu_ln_b[l], sgu_w[l], sgu_b[l], w_o_c[l], w_out[l])
    return _rmsnorm(x2d, final_g, x.dtype).reshape(batch, seq, d)
```

```python
import functools

import jax
import jax.numpy as jnp
from jax import lax
from jax.experimental import pallas as pl
from jax.experimental.pallas import tpu as pltpu

F32 = jnp.float32
BF16 = jnp.bfloat16

N_HEADS = 16
HEAD_DIM = 128
N_KV_GROUPS = 4
HEADS_PER_GROUP = N_HEADS // N_KV_GROUPS
N_NSA_BRANCHES = 3
CMP_BLOCK = 32
CMP_STRIDE = 16
SLC_BLOCK = 64
N_SELECT = 16
WINDOW = 512
ATTN_SCALE = HEAD_DIM ** -0.5
FORCE_SCORE = 1e9
NEG_INF = -1e30
CONV_WIDTH = 31
SGU_CHUNK = 128
RMS_EPS = 1e-6
LN_EPS = 1e-5

LANES = 128
SUBLANES = 8
CONV_HALO = 32
VMEM_LIMIT = 56 * 1024 * 1024

T_NORM = 512
T_PROJ = (1024, 1024)
T_GLU = (1024, 512)
T_MERGE = (512, 512)
T_OUT = (1024, 1024)
T_ATTN = (512, 512)
T_KV_PREP = 2048
T_CONV = (256, 32)
T_SGU = 512


def _params(*semantics):
    return pltpu.CompilerParams(dimension_semantics=semantics, vmem_limit_bytes=VMEM_LIMIT)


def _dot(a, b):
    return jnp.dot(a, b, preferred_element_type=F32)


def _dot_nt(a, b):
    return lax.dot_general(a, b, (((1,), (1,)), ((), ())), preferred_element_type=F32)


def _sigmoid(x):
    return 1.0 / (1.0 + jnp.exp(-x))


def _silu(x):
    return x * _sigmoid(x)


def _rmsnorm_kernel(x_ref, g_ref, o_ref):
    x = x_ref[...]
    ms = jnp.mean(x * x, axis=-1, keepdims=True)
    o_ref[...] = (x * lax.rsqrt(ms + RMS_EPS) * g_ref[...]).astype(o_ref.dtype)


def _rmsnorm(x2d, g, out_dtype, tr=T_NORM):
    t, d = x2d.shape
    return pl.pallas_call(
        _rmsnorm_kernel,
        grid=(t // tr,),
        in_specs=[pl.BlockSpec((tr, d), lambda i: (i, 0)),
                  pl.BlockSpec((1, d), lambda i: (0, 0))],
        out_specs=pl.BlockSpec((tr, d), lambda i: (i, 0)),
        out_shape=jax.ShapeDtypeStruct((t, d), out_dtype),
        compiler_params=_params("parallel"),
        name="rmsnorm",
    )(x2d, g.reshape(1, d))


def _proj_kernel(x_ref, w_ref, o_ref, *, act, scale):
    acc = _dot_nt(x_ref[...], w_ref[...])
    if act == "silu":
        acc = _silu(acc)
    elif act == "sigmoid":
        acc = _sigmoid(acc)
    elif act == "scale":
        acc = acc * scale
    o_ref[...] = acc.astype(o_ref.dtype)


def _pick(j, values):
    r = values[0]
    for idx, v in enumerate(values[1:], 1):
        r = jnp.where(j == idx, v, r)
    return r


ROW_ALIGN = 16


def _row_window(rows, k, start_fn):
    return pl.BlockSpec((pl.Element(rows), pl.Element(k)),
                        lambda *ids: (pl.multiple_of(start_fn(*ids), ROW_ALIGN), 0))


def _tile_starts(slabs, tn):
    assert all(width % tn == 0 for _, width in slabs)
    return tuple(start + r for start, width in slabs for r in range(0, width, tn))


def _proj(x, w_t, starts, act, out_dtype, tm, tn, scale=None):
    t, k = x.shape
    tm = min(tm, t)
    return pl.pallas_call(
        functools.partial(_proj_kernel, act=act, scale=scale),
        grid=(len(starts), t // tm),
        in_specs=[pl.BlockSpec((tm, k), lambda j, i: (i, 0)),
                  _row_window(tn, k, lambda j, i: _pick(j, starts))],
        out_specs=pl.BlockSpec((tm, tn), lambda j, i: (i, j)),
        out_shape=jax.ShapeDtypeStruct((t, len(starts) * tn), out_dtype),
        compiler_params=_params("parallel", "parallel"),
        name="proj_" + act,
    )(x, w_t)


def _glu_kernel(x_ref, wu_ref, wg_ref, o_ref):
    x = x_ref[...]
    o_ref[...] = (_dot_nt(x, wu_ref[...]) * _sigmoid(_dot_nt(x, wg_ref[...]))).astype(o_ref.dtype)


def _glu_proj(x, w_t, row_u, row_g, n, tm, tn):
    t, k = x.shape
    tm, tn = min(tm, t), min(tn, n)
    assert n % tn == 0
    return pl.pallas_call(
        _glu_kernel,
        grid=(n // tn, t // tm),
        in_specs=[pl.BlockSpec((tm, k), lambda j, i: (i, 0)),
                  _row_window(tn, k, lambda j, i: row_u + j * tn),
                  _row_window(tn, k, lambda j, i: row_g + j * tn)],
        out_specs=pl.BlockSpec((tm, tn), lambda j, i: (i, j)),
        out_shape=jax.ShapeDtypeStruct((t, n), F32),
        compiler_params=_params("parallel", "parallel"),
        name="proj_glu",
    )(x, w_t, w_t)


def _compress_kernel(kv_ref, pos_ref, w1_ref, w2_ref, o_ref, *, n_cmp):
    n_pad = o_ref.shape[2]
    half = CMP_BLOCK // 2
    lo = jnp.zeros((n_pad, HEAD_DIM), F32)
    hi = jnp.zeros((n_pad, HEAD_DIM), F32)
    for l in range(half):
        rows = kv_ref[0, pl.ds(l, n_pad, stride=CMP_STRIDE), :]
        a = (rows + pos_ref[0, l:l + 1, :]).astype(BF16)
        b = (rows + pos_ref[0, half + l:half + l + 1, :]).astype(BF16)
        lo = lo + _dot(a, w1_ref[0, l * HEAD_DIM:(l + 1) * HEAD_DIM, :])
        hi = hi + _dot(b, w1_ref[0, (half + l) * HEAD_DIM:(half + l + 1) * HEAD_DIM, :])
    hid = lo + pltpu.roll(hi, n_pad - 1, 0)
    out = _dot(_silu(hid).astype(BF16), w2_ref[0])
    valid = lax.broadcasted_iota(jnp.int32, (n_pad, 1), 0) < n_cmp
    o_ref[0, 0] = jnp.where(valid, out, 0.0).astype(o_ref.dtype)


def _compress(kvc, cmp_pos, cmp_w1, cmp_w2, batch, seq):
    n_pad = seq // CMP_STRIDE
    n_cmp = (seq - CMP_BLOCK) // CMP_STRIDE + 1
    g = N_KV_GROUPS
    return pl.pallas_call(
        functools.partial(_compress_kernel, n_cmp=n_cmp),
        grid=(2, batch, g),
        in_specs=[pl.BlockSpec((1, seq, HEAD_DIM), lambda i, b, gg: (b, 0, i * g + gg)),
                  pl.BlockSpec((1, CMP_BLOCK, HEAD_DIM), lambda i, b, gg: (i, 0, 0)),
                  pl.BlockSpec((1, CMP_BLOCK * HEAD_DIM, HEAD_DIM), lambda i, b, gg: (i, 0, 0)),
                  pl.BlockSpec((1, HEAD_DIM, HEAD_DIM), lambda i, b, gg: (i, 0, 0))],
        out_specs=pl.BlockSpec((1, 1, n_pad, HEAD_DIM), lambda i, b, gg: (i * batch + b, gg, 0, 0)),
        out_shape=jax.ShapeDtypeStruct((2 * batch, g, n_pad, HEAD_DIM), BF16),
        compiler_params=_params("parallel", "parallel", "parallel"),
        name="compress",
    )(kvc, cmp_pos, cmp_w1.astype(BF16), cmp_w2.astype(BF16))


LOG2E = 1.4426950408889634
POS_COL = SLC_BLOCK
N_PIECES = 3
V_ROWS = HEAD_DIM + ROW_ALIGN


def _key_extra(pos):
    n = pos.shape[0]
    col = lax.broadcasted_iota(jnp.int32, (n, HEAD_DIM), 1)
    blk = pos >> (SLC_BLOCK.bit_length() - 1)
    hi = (blk * SLC_BLOCK).astype(F32)
    lo = (pos & (SLC_BLOCK - 1)).astype(F32)
    onehot = jnp.where(col == blk, 1.0, 0.0)
    return jnp.where(col < POS_COL, onehot,
                     jnp.where(col < POS_COL + N_PIECES, hi,
                               jnp.where(col < POS_COL + 2 * N_PIECES, lo, 0.0)))


def _kvprep_kernel(ks_ref, vs_ref, kw_ref, vw_ref, ksa_ref, vst_ref, kwa_ref, vwt_ref):
    ts = ks_ref.shape[1]
    pos = pl.program_id(2) * ts + lax.broadcasted_iota(jnp.int32, (ts, 1), 0)
    extra = _key_extra(pos).astype(BF16)
    for src, dst in ((ks_ref, ksa_ref), (kw_ref, kwa_ref)):
        dst[0, 0, :, 0:HEAD_DIM] = src[0]
        dst[0, 0, :, HEAD_DIM:2 * HEAD_DIM] = extra
    ones_rows = jnp.where(lax.broadcasted_iota(jnp.int32, (V_ROWS - HEAD_DIM, ts), 0) == 0, 1.0, 0.0)
    for src, dst in ((vs_ref, vst_ref), (vw_ref, vwt_ref)):
        dst[0, 0, 0:HEAD_DIM, :] = src[0].astype(F32).T.astype(BF16)
        dst[0, 0, HEAD_DIM:V_ROWS, :] = ones_rows.astype(BF16)


def _kv_prep(kv, batch, seq, ts=T_KV_PREP):
    ts = min(ts, seq)
    g = N_KV_GROUPS
    hd = HEAD_DIM
    ka = jax.ShapeDtypeStruct((batch, g, seq, 2 * hd), BF16)
    vt = jax.ShapeDtypeStruct((batch, g, V_ROWS, seq), BF16)
    ka_spec = pl.BlockSpec((1, 1, ts, 2 * hd), lambda b, gg, i: (b, gg, i, 0))
    vt_spec = pl.BlockSpec((1, 1, V_ROWS, ts), lambda b, gg, i: (b, gg, 0, i))

    def part(p):
        return pl.BlockSpec((1, ts, hd), lambda b, gg, i: (b, i, p * g + gg))

    return pl.pallas_call(
        _kvprep_kernel,
        grid=(batch, g, seq // ts),
        in_specs=[part(0), part(1), part(2), part(3)],
        out_specs=[ka_spec, vt_spec, ka_spec, vt_spec],
        out_shape=[ka, vt, ka, vt],
        compiler_params=_params("parallel", "parallel", "parallel"),
        name="kv_prep",
    )(kv, kv, kv, kv)


def _attn_kernel(q_ref, kc_ref, vc_ref, ksa_ref, vst_ref, kwa_ref, vwt_ref, gate_ref, z_ref, o_ref,
                 kca_ref, vct_ref, *, tq, tk, n_cmp, seq):
    hg = HEADS_PER_GROUP
    g = pl.program_id(1)
    i = pl.program_id(2)
    t0 = i * tq
    n_sblk = seq // SLC_BLOCK
    n_sel = min(N_SELECT, n_sblk)
    n_pad = kc_ref.shape[2]
    nrow = lax.broadcasted_iota(jnp.int32, (n_pad, 1), 0)
    end_c = nrow * CMP_STRIDE + (CMP_BLOCK - 1)

    @pl.when(i == 0)
    def _():
        kca_ref[:, 0:HEAD_DIM] = kc_ref[0, 0]
        kca_ref[:, HEAD_DIM:2 * HEAD_DIM] = _key_extra(end_c).astype(BF16)
        vct_ref[...] = vc_ref[0, 0].astype(F32).T.astype(BF16)

    t_q = t0 + lax.broadcasted_iota(jnp.int32, (1, tq), 1)
    n_tail = HEAD_DIM - POS_COL
    kpos = lax.broadcasted_iota(jnp.int32, (n_tail, 1), 0)

    def slope_rows(h):
        slope2 = jnp.exp2(jnp.full((1, tq), hg * g + (h + 1), jnp.int32).astype(F32) * -0.5) * LOG2E
        piece0 = slope2.astype(BF16).astype(F32)
        rest = slope2 - piece0
        piece1 = rest.astype(BF16).astype(F32)
        piece2 = rest - piece1
        return jnp.where((kpos == 0) | (kpos == N_PIECES), piece0,
                         jnp.where((kpos == 1) | (kpos == N_PIECES + 1), piece1,
                                   jnp.where((kpos == 2) | (kpos == N_PIECES + 2), piece2, 0.0))).astype(BF16)

    q_t = [q_ref[0, :, h * HEAD_DIM:(h + 1) * HEAD_DIM].astype(F32).T.astype(BF16) for h in range(hg)]
    slopes = [slope_rows(h) for h in range(hg)]
    no_bias = jnp.zeros((POS_COL, tq), BF16)
    qa_plain = [jnp.concatenate([q_t[h], no_bias, slopes[h]], axis=0) for h in range(hg)]

    mask_c = (end_c <= t_q) & (nrow < n_cmp)
    kca = kca_ref[...]
    vct = vct_ref[...]
    o_cmp = []
    psum = jnp.zeros((n_pad, tq), F32)
    for h in range(hg):
        s = jnp.where(mask_c, _dot(kca, qa_plain[h]), NEG_INF)
        m = jnp.max(s, axis=0, keepdims=True)
        p = jnp.exp2(s - m)
        l = jnp.sum(p, axis=0, keepdims=True)
        inv = jnp.where(m > 0.5 * NEG_INF, 1.0 / l, 0.0)
        o_cmp.append(_dot(vct, p.astype(BF16)) * inv)
        psum = psum + p * inv

    jrow = lax.broadcasted_iota(jnp.int32, (LANES, n_pad), 0)
    nn = lax.broadcasted_iota(jnp.int32, (LANES, n_pad), 1)
    ovl_t = ((nn * CMP_STRIDE < jrow * SLC_BLOCK + SLC_BLOCK)
             & (nn * CMP_STRIDE + CMP_BLOCK > jrow * SLC_BLOCK)
             & (nn < n_cmp) & (jrow < n_sblk)).astype(F32)
    imp_t = jnp.dot(ovl_t, psum, precision=lax.Precision.HIGHEST, preferred_element_type=F32)
    imp = imp_t[:n_sblk]
    blk = lax.broadcasted_iota(jnp.int32, (n_sblk, tq), 0)
    cur = (t0 + lax.broadcasted_iota(jnp.int32, (n_sblk, tq), 1)) >> (SLC_BLOCK.bit_length() - 1)
    forced = (blk == 0) | (blk == cur) | (blk == cur - 1)
    val = jnp.where(forced, FORCE_SCORE, jnp.where(blk <= cur, imp, NEG_INF))
    sub = 8
    vals = [val[k * sub:(k + 1) * sub] for k in range(n_sblk // sub)]
    ranks = [jnp.zeros((sub, tq), jnp.int32) for _ in vals]
    sub_row = lax.broadcasted_iota(jnp.int32, (sub, tq), 0)
    for jp in range(n_sblk):
        v_jp = jnp.broadcast_to(val[jp:jp + 1, :], (sub, tq))
        for k, v_k in enumerate(vals):
            if (k + 1) * sub <= jp:
                ahead = jnp.where(v_jp > v_k, 1, 0)
            elif k * sub > jp:
                ahead = jnp.where(v_jp >= v_k, 1, 0)
            else:
                ahead = jnp.where(sub_row + k * sub > jp,
                                  jnp.where(v_jp >= v_k, 1, 0), jnp.where(v_jp > v_k, 1, 0))
            ranks[k] = ranks[k] + ahead
    rank = jnp.concatenate(ranks, axis=0)
    sel_bias = jnp.where(rank < n_sel, 0.0, NEG_INF).astype(BF16)
    if n_sblk < POS_COL:
        sel_bias = jnp.concatenate([sel_bias, jnp.zeros((POS_COL - n_sblk, tq), BF16)], axis=0)
    qa_sel = [jnp.concatenate([q_t[h], sel_bias, slopes[h]], axis=0) for h in range(hg)]

    rel = (lax.broadcasted_iota(jnp.int32, (1, tq), 1)
           - lax.broadcasted_iota(jnp.int32, (tk, 1), 0))

    whole = ((0, tk, 0, tq),)

    def run(qa, ka_ref, vt_ref, lo, hi, loop_mask, tail_masks):
        def block(carry, p0, mask_fn, part):
            k_lo, k_hi, q_lo, q_hi = part
            k_a = ka_ref[0, 0, pl.ds(p0 + k_lo, k_hi - k_lo), :]
            v_t = vt_ref[0, 0, :, pl.ds(p0 + k_lo, k_hi - k_lo)]
            mask = None if mask_fn is None else mask_fn(rel[k_lo:k_hi, q_lo:q_hi] + (t0 - p0))
            sc = [_dot(k_a, qa[h][:, q_lo:q_hi]) for h in range(hg)]
            probs, stats = [], []
            for h in range(hg):
                m_prev = carry[h][0][:, q_lo:q_hi]
                s = sc[h] if mask is None else jnp.where(mask, sc[h], NEG_INF)
                m_new = jnp.maximum(m_prev, jnp.max(s, axis=0, keepdims=True))
                probs.append(jnp.exp2((s - m_new).astype(BF16)))
                stats.append((m_new, jnp.exp2(m_prev - m_new)))
            new = []
            for h, (m_new, alpha) in enumerate(stats):
                acc = alpha * carry[h][1][:, q_lo:q_hi] + _dot(v_t, probs[h])
                if (q_lo, q_hi) != (0, tq):
                    m_new, acc = (
                        jnp.concatenate(([old[:, :q_lo]] if q_lo else []) + [part_new]
                                        + ([old[:, q_hi:]] if q_hi < tq else []), axis=1)
                        for old, part_new in zip(carry[h], (m_new, acc)))
                new.append((m_new, acc))
            return tuple(new)

        def tile(carry, j, mask):
            p0 = pl.multiple_of(j * tk, tk)
            mask_fn, parts = (None, whole) if mask is None else mask
            for part in parts:
                carry = block(carry, p0, mask_fn, part)
            return carry

        init = tuple((jnp.full((1, tq), -3e38, F32), jnp.zeros((V_ROWS, tq), F32)) for _ in range(hg))
        carry = lax.fori_loop(lo, hi, lambda j, c: tile(c, j, loop_mask), init)
        for d, mask in enumerate(tail_masks):
            carry = tile(carry, hi + d, mask)
        return [acc[0:HEAD_DIM] * (1.0 / acc[HEAD_DIM:HEAD_DIM + 1]) for _, acc in carry]

    def causal(dist):
        return dist >= 0

    def window(dist):
        return (dist >= 0) & (dist < WINDOW)

    half_k, half_q = tk // 2, tq // 2
    square = tq == tk and half_q % LANES == 0
    diag_parts = ((0, half_k, 0, tq), (half_k, tk, half_q, tq)) if square else whole
    back_parts = ((0, half_k, 0, half_q), (half_k, tk, 0, tq)) if square and tk == WINDOW else whole

    o_slc = run(qa_sel, ksa_ref, vst_ref, 0, t0 // tk, None, [(causal, diag_parts)] * max(tq // tk, 1))

    o_win = run(qa_plain, kwa_ref, vwt_ref, jnp.maximum(t0 - WINDOW + 1, 0) // tk,
                (t0 + tq - 1) // tk, (window, back_parts), [(window, diag_parts)])

    gate_t = gate_ref[0].T
    mine = lax.broadcasted_iota(jnp.int32, (SUBLANES, 1), 0) == g

    def gate(branch, h):
        band = gate_t[(branch * hg + h) * SUBLANES:(branch * hg + h + 1) * SUBLANES]
        return jnp.sum(jnp.where(mine, band, 0.0), axis=0, keepdims=True)

    for h in range(hg):
        o_t = gate(0, h) * o_cmp[h] + gate(1, h) * o_slc[h] + gate(2, h) * o_win[h]
        zc = slice(h * HEAD_DIM, (h + 1) * HEAD_DIM)
        o_ref[0, :, zc] = (o_t.T * z_ref[0, :, zc].astype(F32)).astype(o_ref.dtype)


def _attention(q, kv, kvcmp, gates, zs, batch, seq, tq=T_ATTN[0], tk=T_ATTN[1]):
    tq, tk = min(tq, seq), min(tk, seq)
    g = N_KV_GROUPS
    hd = HEAD_DIM
    gw = HEADS_PER_GROUP * hd
    n_pad = seq // CMP_STRIDE
    n_cmp = (seq - CMP_BLOCK) // CMP_STRIDE + 1
    assert seq // SLC_BLOCK <= POS_COL and (tq % tk == 0 or tk % tq == 0) and WINDOW % tk == 0
    ksa, vst, kwa, vwt = _kv_prep(kv, batch, seq)
    ka_spec = pl.BlockSpec((1, 1, seq, 2 * hd), lambda b, gg, i: (b, gg, 0, 0))
    vt_spec = pl.BlockSpec((1, 1, V_ROWS, seq), lambda b, gg, i: (b, gg, 0, 0))
    return pl.pallas_call(
        functools.partial(_attn_kernel, tq=tq, tk=tk, n_cmp=n_cmp, seq=seq),
        grid=(batch, g, seq // tq),
        in_specs=[pl.BlockSpec((1, tq, gw), lambda b, gg, i: (b, i, gg)),
                  pl.BlockSpec((1, 1, n_pad, hd), lambda b, gg, i: (b, gg, 0, 0)),
                  pl.BlockSpec((1, 1, n_pad, hd), lambda b, gg, i: (batch + b, gg, 0, 0)),
                  ka_spec, vt_spec, ka_spec, vt_spec,
                  pl.BlockSpec((1, tq, LANES), lambda b, gg, i: (b, i, 0)),
                  pl.BlockSpec((1, tq, gw), lambda b, gg, i: (b, i, gg))],
        out_specs=pl.BlockSpec((1, tq, gw), lambda b, gg, i: (b, i, gg)),
        out_shape=jax.ShapeDtypeStruct((batch, seq, N_HEADS * hd), BF16),
        scratch_shapes=[pltpu.VMEM((n_pad, 2 * hd), BF16), pltpu.VMEM((hd, n_pad), BF16)],
        compiler_params=_params("parallel", "parallel", "arbitrary"),
        name="nsa_attention",
    )(q, kvcmp, kvcmp, ksa, vst, kwa, vwt, gates, zs)


def _conv_kernel(cur_ref, halo_ref, w_ref, b_ref, g_ref, beta_ref, z_ref, o_ref, sh_ref, *, rc):
    i = pl.program_id(1)
    ts = cur_ref.shape[1]
    sub = sh_ref.shape[0]
    sh_ref[0, 0:CONV_HALO, :] = jnp.where(i > 0, halo_ref[0], 0.0)
    sh_ref[0, CONV_HALO:CONV_HALO + ts, :] = cur_ref[0]
    n_sh = CONV_HALO + ts - sub
    for s in range(1, sub):
        sh_ref[s, 0:n_sh, :] = sh_ref[0, s:s + n_sh, :]
    first = CONV_HALO - (CONV_WIDTH - 1)
    for c in range(ts // rc):
        acc = jnp.zeros((rc, cur_ref.shape[2]), F32)
        for k in range(CONV_WIDTH):
            s = (first + k) % sub
            base = c * rc + first + k - s
            acc = acc + sh_ref[s, base:base + rc, :] * w_ref[k:k + 1, :]
        y = acc + b_ref[...]
        mu = jnp.mean(y, axis=-1, keepdims=True)
        d = y - mu
        var = jnp.mean(d * d, axis=-1, keepdims=True)
        y = d * lax.rsqrt(var + LN_EPS) * g_ref[...] + beta_ref[...]
        y = _silu(y) * z_ref[0, c * rc:(c + 1) * rc, :].astype(F32)
        o_ref[0, c * rc:(c + 1) * rc, :] = y.astype(o_ref.dtype)


def _conv_module(y_glu, zs, z_block, conv_w, conv_b, ln_g, ln_b, batch, seq, ts=T_CONV[0], rc=T_CONV[1]):
    w = y_glu.shape[-1]
    ts = min(ts, seq)
    hb = ts // CONV_HALO
    vec = lambda a: a.reshape(1, w)
    vspec = pl.BlockSpec((1, w), lambda b, i: (0, 0))
    return pl.pallas_call(
        functools.partial(_conv_kernel, rc=rc),
        grid=(batch, seq // ts),
        in_specs=[pl.BlockSpec((1, ts, w), lambda b, i: (b, i, 0)),
                  pl.BlockSpec((1, CONV_HALO, w), lambda b, i: (b, jnp.maximum(i * hb - 1, 0), 0)),
                  pl.BlockSpec((CONV_WIDTH, w), lambda b, i: (0, 0)),
                  vspec, vspec, vspec,
                  pl.BlockSpec((1, ts, w), lambda b, i: (b, i, z_block))],
        out_specs=pl.BlockSpec((1, ts, w), lambda b, i: (b, i, 0)),
        out_shape=jax.ShapeDtypeStruct((batch, seq, w), BF16),
        scratch_shapes=[pltpu.VMEM((SUBLANES, CONV_HALO + ts, w), F32)],
        compiler_params=_params("parallel", "parallel"),
        name="conv_module",
    )(y_glu, y_glu, conv_w, vec(conv_b), vec(ln_g), vec(ln_b), zs)


def _sgu_kernel(u_ref, v_ref, z_ref, g_ref, beta_ref, w_ref, bt_ref, o_ref):
    ts, width = v_ref.shape
    n_groups = w_ref.shape[0]
    gw = width // n_groups
    v = v_ref[...]
    mu = jnp.mean(v, axis=-1, keepdims=True)
    d = v - mu
    var = jnp.mean(d * d, axis=-1, keepdims=True)
    vn = (d * lax.rsqrt(var + LN_EPS) * g_ref[...] + beta_ref[...]).astype(BF16)
    tri = (lax.broadcasted_iota(jnp.int32, (SGU_CHUNK, SGU_CHUNK), 0)
           >= lax.broadcasted_iota(jnp.int32, (SGU_CHUNK, SGU_CHUNK), 1))
    for gi in range(n_groups):
        w_g = jnp.where(tri, w_ref[gi], 0.0).astype(BF16)
        cols = slice(gi * gw, (gi + 1) * gw)
        for c in range(ts // SGU_CHUNK):
            rws = slice(c * SGU_CHUNK, (c + 1) * SGU_CHUNK)
            s = _dot(w_g, vn[rws, cols]) + bt_ref[:, gi:gi + 1]
            out = u_ref[rws, cols].astype(F32) * s * z_ref[rws, cols].astype(F32)
            o_ref[rws, cols] = out.astype(o_ref.dtype)


def _sgu_module(u_src, u_block, v_src, v_block, zs, z_block, ln_g, ln_b, sgu_w, sgu_b, ts=T_SGU):
    t = v_src.shape[0]
    n_groups = sgu_w.shape[0]
    width = n_groups * sgu_w.shape[1]
    ts = min(ts, t)
    vspec = pl.BlockSpec((1, width), lambda i: (0, 0))
    return pl.pallas_call(
        _sgu_kernel,
        grid=(t // ts,),
        in_specs=[pl.BlockSpec((ts, width), lambda i: (i, u_block)),
                  pl.BlockSpec((ts, width), lambda i: (i, v_block)),
                  pl.BlockSpec((ts, width), lambda i: (i, z_block)),
                  vspec, vspec,
                  pl.BlockSpec(sgu_w.shape, lambda i: (0, 0, 0)),
                  pl.BlockSpec((SGU_CHUNK, n_groups), lambda i: (0, 0))],
        out_specs=pl.BlockSpec((ts, width), lambda i: (i, 0)),
        out_shape=jax.ShapeDtypeStruct((t, width), BF16),
        compiler_params=_params("parallel"),
        name="sgu_module",
    )(u_src, v_src, zs, ln_g.reshape(1, width), ln_b.reshape(1, width), sgu_w, sgu_b.T)


def _merge_kernel(h_ref, oa_ref, ob_ref, oc_ref, wm0_ref, wm1_ref, wm2_ref,
                  woa_ref, wob_ref, woc_ref, y_ref):
    h = h_ref[...]
    y = _sigmoid(_dot_nt(h, wm0_ref[...])) * _dot(oa_ref[...], woa_ref[...])
    y = y + _sigmoid(_dot_nt(h, wm1_ref[...])) * _dot(ob_ref[...], wob_ref[...])
    y = y + _sigmoid(_dot_nt(h, wm2_ref[...])) * _dot(oc_ref[...], woc_ref[...])
    y_ref[...] = y.astype(y_ref.dtype)


def _merge(h, oa, ob, oc, w_t, row, w_o_a, w_o_b, w_o_c, tm=T_MERGE[0], tn=T_MERGE[1]):
    t, d = h.shape
    tm, tn = min(tm, t), min(tn, d)
    nb = d // tn

    def act(a):
        return pl.BlockSpec((tm, a.shape[1]), lambda j, i: (i, 0))

    def wm(branch):
        return _row_window(tn, d, lambda j, i: row + branch * d + j * tn)

    def wo(a):
        return pl.BlockSpec((a.shape[0], tn), lambda j, i: (0, j))

    return pl.pallas_call(
        _merge_kernel,
        grid=(nb, t // tm),
        in_specs=[act(h), act(oa), act(ob), act(oc), wm(0), wm(1), wm(2),
                  wo(w_o_a), wo(w_o_b), wo(w_o_c)],
        out_specs=pl.BlockSpec((tm, tn), lambda j, i: (i, j)),
        out_shape=jax.ShapeDtypeStruct((t, d), BF16),
        compiler_params=_params("parallel", "parallel"),
        name="merge",
    )(h, oa, ob, oc, w_t, w_t, w_t, w_o_a, w_o_b, w_o_c)


def _out_kernel(y_ref, w_ref, x_ref, o_ref):
    o_ref[...] = x_ref[...] + _dot(y_ref[...], w_ref[...])


def _out_proj(y, w, x, tm=T_OUT[0], tn=T_OUT[1]):
    t, d = y.shape
    n = w.shape[1]
    tm, tn = min(tm, t), min(tn, n)
    return pl.pallas_call(
        _out_kernel,
        grid=(t // tm, n // tn),
        in_specs=[pl.BlockSpec((tm, d), lambda i, j: (i, 0)),
                  pl.BlockSpec((d, tn), lambda i, j: (0, j)),
                  pl.BlockSpec((tm, tn), lambda i, j: (i, j))],
        out_specs=pl.BlockSpec((tm, tn), lambda i, j: (i, j)),
        out_shape=jax.ShapeDtypeStruct((t, n), F32),
        compiler_params=_params("parallel", "parallel"),
        name="out_proj",
    )(y, w, x)


def _layer(x2d, batch, seq, layer, norm_g, w_in, cmp_pos, cmp_w1, cmp_w2, w_o_a, conv_w, conv_b,
           conv_ln_g, conv_ln_b, w_o_b, sgu_ln_g, sgu_ln_b, sgu_w, sgu_b, w_o_c, w_out):
    d = x2d.shape[1]
    a_w = w_o_a.shape[0]
    b_w = w_o_b.shape[0]
    c_w = w_o_c.shape[0]
    kv_w = N_KV_GROUPS * HEAD_DIM
    n_gate = N_NSA_BRANCHES * N_HEADS
    hg = HEADS_PER_GROUP

    o_q = 0
    o_kv = o_q + a_w
    o_gate = o_kv + 6 * kv_w
    o_za = o_gate + n_gate
    o_b = o_za + a_w
    o_zb = o_b + 2 * b_w
    o_c = o_zb + b_w
    o_zc = o_c + 2 * c_w
    o_m = o_zc + c_w

    w_t = jnp.swapaxes(w_in, 1, 2)[layer]
    wp = w_t.astype(BF16)
    wg = w_t[o_gate:o_gate + n_gate].reshape(N_NSA_BRANCHES, N_KV_GROUPS, hg, d)
    wg = wg.transpose(0, 2, 1, 3).reshape(N_NSA_BRANCHES * hg, N_KV_GROUPS, d)
    wg = jnp.pad(wg, ((0, 0), (0, SUBLANES - N_KV_GROUPS), (0, 0))).reshape(N_NSA_BRANCHES * hg * SUBLANES, d)
    wg = jnp.pad(wg, ((0, LANES - wg.shape[0]), (0, 0))).astype(BF16)

    tm, tn = T_PROJ
    h = _rmsnorm(x2d, norm_g, BF16)
    p_q = _proj(h, wp, _tile_starts([(o_q, a_w)], tn), "scale", BF16, tm, tn, scale=ATTN_SCALE * LOG2E)
    p_bf = _proj(h, wp, _tile_starts([(o_kv + 2 * kv_w, 4 * kv_w), (o_c, c_w)], tn),
                 "none", BF16, tm, tn)
    p_f32 = _proj(h, wp, _tile_starts([(o_kv, 2 * kv_w), (o_c + c_w, c_w)], tn),
                  "none", F32, tm, tn)
    p_silu = _proj(h, wp, _tile_starts([(o_za, a_w), (o_zb, b_w), (o_zc, c_w)], tn),
                   "silu", BF16, tm, tn)
    p_gate = _proj(h, wg, (0,), "sigmoid", F32, tm, LANES)
    y_glu = _glu_proj(h, wp, o_b, o_b + b_w, b_w, *T_GLU)

    def b3(a):
        return a.reshape(batch, seq, a.shape[-1])

    kvcmp = _compress(b3(p_f32), cmp_pos, cmp_w1, cmp_w2, batch, seq)
    o_a = _attention(b3(p_q), b3(p_bf), kvcmp, b3(p_gate), b3(p_silu), batch, seq)
    o_b_ = _conv_module(b3(y_glu), b3(p_silu), a_w // b_w, conv_w, conv_b, conv_ln_g, conv_ln_b,
                        batch, seq)
    o_c_ = _sgu_module(p_bf, 4 * kv_w // c_w, p_f32, 2 * kv_w // c_w,
                       p_silu, (a_w + b_w) // c_w, sgu_ln_g, sgu_ln_b, sgu_w, sgu_b)
    y = _merge(h, o_a.reshape(-1, a_w), o_b_.reshape(-1, b_w), o_c_, wp, o_m,
               w_o_a.astype(BF16), w_o_b.astype(BF16), w_o_c.astype(BF16))
    return _out_proj(y, w_out.astype(BF16), x2d)


def kernel(x, norm_g, w_in, cmp_pos, cmp_w1, cmp_w2, w_o_a, conv_w, conv_b, conv_ln_g, conv_ln_b,
           w_o_b, sgu_ln_g, sgu_ln_b, sgu_w, sgu_b, w_o_c, w_out, final_g):
    batch, seq, d = x.shape
    x2d = x.reshape(batch * seq, d)
    for l in range(norm_g.shape[0]):
        x2d = _layer(x2d, batch, seq, l, norm_g[l], w_in, cmp_pos[l], cmp_w1[l], cmp_w2[l],
                     w_o_a[l], conv_w[l], conv_b[l], conv_ln_g[l], conv_ln_b[l], w_o_b[l],
                     sgu_ln_g[l], sgu_ln_b[l], sgu_w[l], sgu_b[l], w_o_c[l], w_out[l])
    return _rmsnorm(x2d, final_g, x.dtype).reshape(batch, seq, d)
```
